```python
import jax, jax.numpy as jnp
from jax import lax
import numpy as np

D_MODEL = 1024
BATCH = 2
SEQ = 16384
DEPTH = 2

CTX_LEN = 256
GRID_W = 64
N_MOD = 9
D_FF = 2816
CONV_DIM = D_MODEL
CONV_K = 31
SSD_EXPAND = 2
D_INNER = SSD_EXPAND * D_MODEL
HEAD_DIM = 64
N_HEADS = D_INNER // HEAD_DIM
N_GROUPS = 8
HEADS_PER_GROUP = N_HEADS // N_GROUPS
D_STATE = 128
SSD_CONV_K = 5
CHUNK = 128
N_DIR = 2
XBC_DIM = D_INNER + 2 * N_GROUPS * D_STATE
N_IN = 2 * CONV_DIM + D_INNER + XBC_DIM + N_DIR * N_HEADS + 2 * D_MODEL
EPS = 1e-6

kernel_name = "hybrid_conformer_ssd_prefix_dit"


def rms_norm(x, g):
    xf = x.astype(jnp.float32)
    xf = xf * lax.rsqrt(jnp.mean(jnp.square(xf), axis=-1, keepdims=True) + EPS)
    return (xf * g.astype(jnp.float32)).astype(x.dtype)


def layer_norm(x, g, b):
    xf = x.astype(jnp.float32)
    mu = jnp.mean(xf, axis=-1, keepdims=True)
    xc = xf - mu
    xf = xc * lax.rsqrt(jnp.mean(jnp.square(xc), axis=-1, keepdims=True) + EPS)
    return (xf * g.astype(jnp.float32) + b.astype(jnp.float32)).astype(x.dtype)


def modulate(h, shift, scale):
    return h * (1.0 + scale) + shift


def half_ffn(t, shift, scale, gate, norm_g, w_gate, w_up, w_down):
    h = modulate(rms_norm(t, norm_g), shift, scale)
    y = (jax.nn.silu(h @ w_gate) * (h @ w_up)) @ w_down
    return t + 0.5 * gate * y


def dwconv(u, w, b):
    k = w.shape[0]
    out = lax.conv_general_dilated(
        u, w[:, None, :], window_strides=(1,), padding=[(k // 2, k // 2)],
        dimension_numbers=('NWC', 'WIO', 'NWC'), feature_group_count=u.shape[-1])
    return out + b


def split_in(p):
    s1 = 2 * CONV_DIM
    s2 = s1 + D_INNER
    s3 = s2 + XBC_DIM
    s4 = s3 + N_DIR * N_HEADS
    glu, z, xbc, dt_raw, gate_raw = jnp.split(p, [s1, s2, s3, s4], axis=-1)
    return glu, z, xbc, dt_raw, gate_raw


def conv_branch(glu, rows, dw_w, dw_b, ln_g, ln_b, w_proj):
    bsz, seq_len, _ = glu.shape
    a, b = jnp.split(glu, 2, axis=-1)
    u = a * jax.nn.sigmoid(b)
    u = u.reshape(bsz * rows, seq_len // rows, CONV_DIM)
    u = dwconv(u, dw_w, dw_b).reshape(bsz, seq_len, CONV_DIM)
    u = jax.nn.silu(layer_norm(u, ln_g, ln_b))
    return u @ w_proj


def ssd_features(xbc_raw, dt_raw, conv_w, conv_b, dt_bias):
    bsz, seq_len, _ = xbc_raw.shape
    xbc = jax.nn.silu(dwconv(xbc_raw, conv_w, conv_b))
    xs, bm, cm = jnp.split(xbc, [D_INNER, D_INNER + N_GROUPS * D_STATE], axis=-1)
    xs = xs.reshape(bsz, seq_len, N_GROUPS, HEADS_PER_GROUP, HEAD_DIM)
    bm = bm.reshape(bsz, seq_len, N_GROUPS, D_STATE)
    cm = cm.reshape(bsz, seq_len, N_GROUPS, D_STATE)
    dt = jax.nn.softplus(dt_raw.reshape(bsz, seq_len, N_DIR, N_GROUPS, HEADS_PER_GROUP)
                         + dt_bias.reshape(N_DIR, N_GROUPS, HEADS_PER_GROUP))
    return xs, bm, cm, dt


def ssd_scan(xs, dt, a, bm, cm, s0):
    bsz, seq_len = xs.shape[:2]
    nc = seq_len // CHUNK
    f32 = jnp.float32
    dtf = dt.astype(f32)
    xdt = (xs.astype(f32) * dtf[..., None]).reshape(bsz, nc, CHUNK, N_GROUPS, HEADS_PER_GROUP, HEAD_DIM)
    adt = (dtf * a.astype(f32)).reshape(bsz, nc, CHUNK, N_GROUPS, HEADS_PER_GROUP)
    bq = bm.astype(f32).reshape(bsz, nc, CHUNK, N_GROUPS, D_STATE)
    cq = cm.astype(f32).reshape(bsz, nc, CHUNK, N_GROUPS, D_STATE)
    acum = jnp.cumsum(adt, axis=2)
    lower = jnp.tril(jnp.ones((CHUNK, CHUNK), dtype=bool))[None, None, :, :, None, None]
    seg = acum[:, :, :, None] - acum[:, :, None, :]
    decay = jnp.exp(jnp.where(lower, seg, -jnp.inf))
    cb = jnp.einsum('bclgn,bcsgn->bclsg', cq, bq)
    y_diag = jnp.einsum('bclsgh,bcsghp->bclghp', cb[..., None] * decay, xdt)
    to_end = jnp.exp(acum[:, :, -1:] - acum)
    chunk_states = jnp.einsum('bcsgn,bcsghp->bcghpn', bq, xdt * to_end[..., None])
    chunk_decay = jnp.exp(acum[:, :, -1])

    def step(state, inp):
        dec, new = inp
        return dec[..., None, None] * state + new, state

    s_final, s_in = lax.scan(step, s0.astype(f32),
                             (jnp.moveaxis(chunk_decay, 1, 0), jnp.moveaxis(chunk_states, 1, 0)))
    s_in = jnp.moveaxis(s_in, 0, 1)
    y_off = jnp.einsum('bclgn,bcghpn->bclghp', cq, s_in) * jnp.exp(acum)[..., None]
    y = (y_diag + y_off).reshape(bsz, seq_len, N_GROUPS, HEADS_PER_GROUP, HEAD_DIM)
    return y.astype(xs.dtype), s_final


def ssd_bidir(xs, bm, cm, dt, a, s0_f, s0_b):
    flip = lambda t: jnp.flip(t, axis=1)
    y_f, s_f = ssd_scan(xs, dt[:, :, 0], a[0], bm, cm, s0_f)
    y_b, s_b = ssd_scan(flip(xs), flip(dt[:, :, 1]), a[1], flip(bm), flip(cm), s0_b)
    return y_f + flip(y_b), s_f, s_b


def ssd_output(y, xs, z, d_skip, norm_g, w_proj):
    bsz, seq_len = y.shape[:2]
    y = y + d_skip.reshape(N_GROUPS, HEADS_PER_GROUP, 1) * xs
    y = y.reshape(bsz, seq_len, D_INNER) * jax.nn.silu(z)
    y = rms_norm(y.reshape(bsz, seq_len, N_GROUPS, D_INNER // N_GROUPS),
                 norm_g.reshape(N_GROUPS, D_INNER // N_GROUPS)).reshape(bsz, seq_len, D_INNER)
    return y @ w_proj


def mixer_merge(glu, z, gate_raw, y_ssd, xs, rows, dw_w, dw_b, ln_g, ln_b, conv_w_out,
                d_skip, norm_g, ssd_w_out, w_out):
    y_conv = conv_branch(glu, rows, dw_w, dw_b, ln_g, ln_b, conv_w_out)
    y_s = ssd_output(y_ssd, xs, z, d_skip, norm_g, ssd_w_out)
    g_conv, g_ssd = jnp.split(jax.nn.sigmoid(gate_raw), 2, axis=-1)
    return (g_conv * y_conv + g_ssd * y_s) @ w_out


def setup_inputs(seed: int = 0) -> dict:
    key = jax.random.key(seed)
    ks = iter(jax.random.split(key, 40))
    nrm = lambda shape, scale: scale * jax.random.normal(next(ks), shape, jnp.float32)
    gain = lambda shape: 1.0 + 0.01 * jax.random.normal(next(ks), shape, jnp.float32)
    L = DEPTH
    dt0 = jnp.exp(jax.random.uniform(next(ks), (L, N_DIR, N_HEADS), jnp.float32,
                                     np.log(1e-3), np.log(1e-1)))
    dt_bias = dt0 + jnp.log(-jnp.expm1(-dt0))
    a_log = jnp.log(jax.random.uniform(next(ks), (L, N_DIR, N_HEADS), jnp.float32, 1.0, 16.0))
    return {
        "x": nrm((BATCH, SEQ, D_MODEL), 1.0),
        "c": nrm((BATCH, D_MODEL), 1.0),
        "ctx": nrm((BATCH, CTX_LEN, D_MODEL), 1.0),
        "c_ctx": nrm((D_MODEL,), 1.0),
        "w_ada": nrm((L, D_MODEL, N_MOD * D_MODEL), 0.5 * D_MODEL ** -0.5),
        "b_ada": nrm((L, N_MOD * D_MODEL), 0.01),
        "ffn1_norm": gain((L, D_MODEL)),
        "ffn1_w_gate": nrm((L, D_MODEL, D_FF), D_MODEL ** -0.5),
        "ffn1_w_up": nrm((L, D_MODEL, D_FF), D_MODEL ** -0.5),
        "ffn1_w_down": nrm((L, D_FF, D_MODEL), D_FF ** -0.5),
        "mix_norm": gain((L, D_MODEL)),
        "w_in": nrm((L, D_MODEL, N_IN), D_MODEL ** -0.5),
        "b_in": nrm((L, N_IN), 0.01),
        "conv_dw_w": nrm((L, CONV_K, CONV_DIM), CONV_K ** -0.5),
        "conv_dw_b": nrm((L, CONV_DIM), 0.01),
        "conv_ln_g": gain((L, CONV_DIM)),
        "conv_ln_b": nrm((L, CONV_DIM), 0.01),
        "conv_w_out": nrm((L, CONV_DIM, D_MODEL), CONV_DIM ** -0.5),
        "ssd_conv_w": nrm((L, SSD_CONV_K, XBC_DIM), SSD_CONV_K ** -0.5),
        "ssd_conv_b": nrm((L, XBC_DIM), 0.01),
        "ssd_dt_bias": dt_bias,
        "ssd_a_log": a_log,
        "ssd_d": gain((L, N_HEADS)),
        "ssd_norm": gain((L, D_INNER)),
        "ssd_w_out": nrm((L, D_INNER, D_MODEL), D_INNER ** -0.5),
        "w_out": nrm((L, D_MODEL, D_MODEL), D_MODEL ** -0.5),
        "ffn2_norm": gain((L, D_MODEL)),
        "ffn2_w_gate": nrm((L, D_MODEL, D_FF), D_MODEL ** -0.5),
        "ffn2_w_up": nrm((L, D_MODEL, D_FF), D_MODEL ** -0.5),
        "ffn2_w_down": nrm((L, D_FF, D_MODEL), D_FF ** -0.5),
        "final_norm": gain((D_MODEL,)),
    }


def reference(x, c, ctx, c_ctx, w_ada, b_ada, ffn1_norm, ffn1_w_gate, ffn1_w_up, ffn1_w_down,
              mix_norm, w_in, b_in, conv_dw_w, conv_dw_b, conv_ln_g, conv_ln_b, conv_w_out,
              ssd_conv_w, ssd_conv_b, ssd_dt_bias, ssd_a_log, ssd_d, ssd_norm, ssd_w_out, w_out,
              ffn2_norm, ffn2_w_gate, ffn2_w_up, ffn2_w_down, final_norm):
    bsz, seq_len = x.shape[:2]
    rows = seq_len // GRID_W
    c_act = jax.nn.silu(c)
    cc_act = jax.nn.silu(c_ctx)
    for i in range(DEPTH):
        update_ctx = i < DEPTH - 1
        mod_x = jnp.split((c_act @ w_ada[i] + b_ada[i])[:, None, :], N_MOD, axis=-1)
        mod_c = jnp.split(cc_act @ w_ada[i] + b_ada[i], N_MOD, axis=-1)

        x = half_ffn(x, mod_x[0], mod_x[1], mod_x[2], ffn1_norm[i], ffn1_w_gate[i], ffn1_w_up[i], ffn1_w_down[i])
        ctx = half_ffn(ctx, mod_c[0], mod_c[1], mod_c[2], ffn1_norm[i], ffn1_w_gate[i], ffn1_w_up[i], ffn1_w_down[i])

        px = modulate(rms_norm(x, mix_norm[i]), mod_x[3], mod_x[4]) @ w_in[i] + b_in[i]
        pc = modulate(rms_norm(ctx, mix_norm[i]), mod_c[3], mod_c[4]) @ w_in[i] + b_in[i]
        glu_x, z_x, xbc_x, dt_x, gate_x = split_in(px)
        glu_c, z_c, xbc_c, dt_c, gate_c = split_in(pc)
        a = -jnp.exp(ssd_a_log[i].astype(jnp.float32)).reshape(N_DIR, N_GROUPS, HEADS_PER_GROUP)

        xs_c, bm_c, cm_c, dtv_c = ssd_features(xbc_c, dt_c, ssd_conv_w[i], ssd_conv_b[i], ssd_dt_bias[i])
        s0 = jnp.zeros((bsz, N_GROUPS, HEADS_PER_GROUP, HEAD_DIM, D_STATE), jnp.float32)
        y_c, s_f, s_b = ssd_bidir(xs_c, bm_c, cm_c, dtv_c, a, s0, s0)
        xs_x, bm_x, cm_x, dtv_x = ssd_features(xbc_x, dt_x, ssd_conv_w[i], ssd_conv_b[i], ssd_dt_bias[i])
        y_x, _, _ = ssd_bidir(xs_x, bm_x, cm_x, dtv_x, a, s_f, s_b)

        mix_x = mixer_merge(glu_x, z_x, gate_x, y_x, xs_x, rows, conv_dw_w[i], conv_dw_b[i], conv_ln_g[i],
                            conv_ln_b[i], conv_w_out[i], ssd_d[i], ssd_norm[i], ssd_w_out[i], w_out[i])
        x = x + mod_x[5] * mix_x
        if update_ctx:
            mix_c = mixer_merge(glu_c, z_c, gate_c, y_c, xs_c, 1, conv_dw_w[i], conv_dw_b[i], conv_ln_g[i],
                                conv_ln_b[i], conv_w_out[i], ssd_d[i], ssd_norm[i], ssd_w_out[i], w_out[i])
            ctx = ctx + mod_c[5] * mix_c

        x = half_ffn(x, mod_x[6], mod_x[7], mod_x[8], ffn2_norm[i], ffn2_w_gate[i], ffn2_w_up[i], ffn2_w_down[i])
        if update_ctx:
            ctx = half_ffn(ctx, mod_c[6], mod_c[7], mod_c[8], ffn2_norm[i], ffn2_w_gate[i], ffn2_w_up[i], ffn2_w_down[i])
    return rms_norm(x, final_norm)
```

```python
import functools

import jax
import jax.numpy as jnp
from jax import lax
from jax.experimental import pallas as pl
from jax.experimental.pallas import tpu as pltpu

F32 = jnp.float32
BF16 = jnp.bfloat16

D_MODEL = 1024
N_MOD = 9
D_FF = 2816
CONV_DIM = D_MODEL
CONV_K = 31
D_INNER = 2 * D_MODEL
HEAD_DIM = 64
N_HEADS = D_INNER // HEAD_DIM
N_GROUPS = 8
HEADS_PER_GROUP = N_HEADS // N_GROUPS
GROUP_W = HEADS_PER_GROUP * HEAD_DIM
D_STATE = 128
SSD_CONV_K = 5
CHUNK = 128
N_DIR = 2
XBC_DIM = D_INNER + 2 * N_GROUPS * D_STATE
GRID_W = 64
EPS = 1e-6

LANES = 128
BF16_ROWS = 16
FF_CHUNK = 256
CONV_PAD = 16
VMEM_LIMIT = 58 * 1024 * 1024


def _dot(a, b):
    return jnp.dot(a, b, preferred_element_type=F32)


def _sigmoid(v):
    return 1.0 / (1.0 + jnp.exp(-v))


def _const_spec(shape):
    nd = len(shape)
    return pl.BlockSpec(shape, lambda *_: (0,) * nd, pipeline_mode=pl.Buffered(1))


def _params(n_axes=1):
    return pltpu.CompilerParams(dimension_semantics=("arbitrary",) * n_axes,
                                vmem_limit_bytes=VMEM_LIMIT)


def _mods_kernel(c_ref, w_ref, b_ref, o_ref):
    c = c_ref[...]
    act = c * _sigmoid(c)
    o_ref[0] = jnp.dot(act, w_ref[0], precision=lax.Precision.HIGHEST,
                       preferred_element_type=F32) + b_ref[0]


def _mods(cvec, w_ada, b_ada):
    depth, _, n_out = w_ada.shape
    tn = D_MODEL
    return pl.pallas_call(
        _mods_kernel,
        grid=(depth, n_out // tn),
        in_specs=[pl.BlockSpec((8, D_MODEL), lambda l, j: (0, 0)),
                  pl.BlockSpec((1, D_MODEL, tn), lambda l, j: (l, 0, j)),
                  pl.BlockSpec((1, 1, tn), lambda l, j: (l, 0, j))],
        out_specs=pl.BlockSpec((1, 8, tn), lambda l, j: (l, 0, j)),
        out_shape=jax.ShapeDtypeStruct((depth, 8, n_out), F32),
        compiler_params=_params(2),
        name="ada_mods",
    )(cvec, w_ada, b_ada.reshape(depth, 1, n_out))


def _rms(v, g):
    ms = jnp.mean(v * v, axis=-1, keepdims=True)
    return v * lax.rsqrt(ms + EPS) * g


def _ffn_kernel(*refs, k0, post, k1):
    if post == "mod":
        x_ref, mod_ref, g_ref, wg_ref, wu_ref, wd_ref, g2_ref, o_ref, h_ref = refs
    elif post == "final":
        x_ref, mod_ref, g_ref, wg_ref, wu_ref, wd_ref, g2_ref, o_ref = refs
    else:
        x_ref, mod_ref, g_ref, wg_ref, wu_ref, wd_ref, o_ref = refs
    x = x_ref[...]
    shift = mod_ref[0, k0:k0 + 1, :]
    scale = mod_ref[0, k0 + 1:k0 + 2, :]
    gate = mod_ref[0, k0 + 2:k0 + 3, :]
    hb = (_rms(x, g_ref[...]) * (1.0 + scale) + shift).astype(BF16)
    acc = jnp.zeros(x.shape, F32)
    for c in range(D_FF // FF_CHUNK):
        sl = slice(c * FF_CHUNK, (c + 1) * FF_CHUNK)
        g = _dot(hb, wg_ref[:, sl])
        u = _dot(hb, wu_ref[:, sl])
        a = (g * _sigmoid(g)) * u
        acc = acc + _dot(a.astype(BF16), wd_ref[sl, :])
    y = x + 0.5 * gate * acc
    if post == "final":
        o_ref[...] = _rms(y, g2_ref[...])
        return
    o_ref[...] = y
    if post == "mod":
        shift2 = mod_ref[0, k1:k1 + 1, :]
        scale2 = mod_ref[0, k1 + 1:k1 + 2, :]
        h_ref[...] = (_rms(y, g2_ref[...]) * (1.0 + scale2) + shift2).astype(BF16)


def _ffn(x, mods, mod_map, norm_g, wg, wu, wd, *, tm, k0, post="none", g2=None, k1=0):
    t = x.shape[0]
    row = pl.BlockSpec((tm, D_MODEL), lambda i: (i, 0))
    in_specs = [row,
                pl.BlockSpec((1, N_MOD, D_MODEL), mod_map),
                _const_spec((1, D_MODEL)),
                _const_spec((D_MODEL, D_FF)), _const_spec((D_MODEL, D_FF)),
                _const_spec((D_FF, D_MODEL))]
    args = [x, mods, norm_g.reshape(1, D_MODEL), wg, wu, wd]
    if post != "none":
        in_specs.append(_const_spec((1, D_MODEL)))
        args.append(g2.reshape(1, D_MODEL))
    if post == "mod":
        out_specs = [row, row]
        out_shape = [jax.ShapeDtypeStruct((t, D_MODEL), F32), jax.ShapeDtypeStruct((t, D_MODEL), BF16)]
    else:
        out_specs = row
        out_shape = jax.ShapeDtypeStruct((t, D_MODEL), F32)
    return pl.pallas_call(
        functools.partial(_ffn_kernel, k0=k0, post=post, k1=k1),
        grid=(t // tm,), in_specs=in_specs, out_specs=out_specs, out_shape=out_shape,
        compiler_params=_params(), name="half_ffn_" + post,
    )(*args)


def _conv_kernel(h_ref, wa_ref, wb_ref, ba_ref, bb_ref, dww_ref, dwb_ref, lng_ref, lnb_ref, wo_ref,
                 o_ref, upad_ref, conv_ref, *, seg):
    tm = h_ref.shape[0]
    nseg = tm // seg
    hb = h_ref[...]
    a = _dot(hb, wa_ref[...]) + ba_ref[...]
    b = _dot(hb, wb_ref[...]) + bb_ref[...]
    u = a * _sigmoid(b)
    zpad = jnp.zeros((nseg, CONV_PAD, CONV_DIM), F32)
    upad_ref[:, 0:CONV_PAD, :] = zpad
    upad_ref[:, CONV_PAD + seg:, :] = zpad
    upad_ref[:, CONV_PAD:CONV_PAD + seg, :] = u.reshape(nseg, seg, CONV_DIM)

    n_lane_blk = CONV_DIM // LANES
    first = CONV_PAD - CONV_K // 2

    def body(i, carry):
        s = i // n_lane_blk
        cl = pl.multiple_of((i % n_lane_blk) * LANES, LANES)
        acc = jnp.zeros((seg, LANES), F32)
        for k in range(CONV_K):
            acc = acc + upad_ref[s, pl.ds(first + k, seg), pl.ds(cl, LANES)] * dww_ref[k:k + 1, pl.ds(cl, LANES)]
        conv_ref[pl.ds(pl.multiple_of(s * seg, seg), seg), pl.ds(cl, LANES)] = acc + dwb_ref[:, pl.ds(cl, LANES)]
        return carry

    lax.fori_loop(0, nseg * n_lane_blk, body, 0)

    v = conv_ref[...]
    mu = jnp.mean(v, axis=-1, keepdims=True)
    vc = v - mu
    var = jnp.mean(vc * vc, axis=-1, keepdims=True)
    vn = vc * lax.rsqrt(var + EPS) * lng_ref[...] + lnb_ref[...]
    act = vn * _sigmoid(vn)
    o_ref[...] = _dot(act.astype(BF16), wo_ref[...]).astype(BF16)


def _conv_branch(h, wa, wb, ba, bb, dww, dwb, lng, lnb, wo, *, tm, seg):
    t = h.shape[0]
    row = pl.BlockSpec((tm, D_MODEL), lambda i: (i, 0))
    vec = _const_spec((1, CONV_DIM))
    sq = _const_spec((D_MODEL, CONV_DIM))
    return pl.pallas_call(
        functools.partial(_conv_kernel, seg=seg),
        grid=(t // tm,),
        in_specs=[row, sq, sq, vec, vec, _const_spec((CONV_K, CONV_DIM)), vec, vec, vec, sq],
        out_specs=row,
        out_shape=jax.ShapeDtypeStruct((t, D_MODEL), BF16),
        scratch_shapes=[pltpu.VMEM((tm // seg, seg + 2 * CONV_PAD, CONV_DIM), F32),
                        pltpu.VMEM((tm, CONV_DIM), F32)],
        compiler_params=_params(), name="conv_branch",
    )(h, wa, wb, ba.reshape(1, -1), bb.reshape(1, -1), dww, dwb.reshape(1, -1), lng.reshape(1, -1),
      lnb.reshape(1, -1), wo)


FEAT_ROWS = 64


def _feat_kernel(hp_ref, h_ref, hn_ref, wx_ref, bx_ref, wdt_ref, bdt_ref, cw_ref, cb_ref,
                 xbc_ref, dt_ref, raw_ref, *, lseq):
    tm = h_ref.shape[0]
    halo = BF16_ROWS
    i = pl.program_id(0)
    hfull = jnp.concatenate([hp_ref[...], h_ref[...], hn_ref[...]], axis=0)
    raw = _dot(hfull, wx_ref[...]) + bx_ref[...]
    t0 = i * tm
    pos = t0 - halo + lax.broadcasted_iota(jnp.int32, (tm + 2 * halo, 1), 0)
    seq_lo = (t0 // lseq) * lseq
    valid = (pos >= seq_lo) & (pos < seq_lo + lseq)
    raw_ref[...] = jnp.where(valid, raw, 0.0)

    n_lane_blk = XBC_DIM // LANES
    first = halo - SSD_CONV_K // 2

    def body(j, carry):
        cl = pl.multiple_of(j * LANES, LANES)
        lanes = pl.ds(cl, LANES)
        for r0 in range(0, tm, FEAT_ROWS):
            acc = jnp.zeros((FEAT_ROWS, LANES), F32)
            for k in range(SSD_CONV_K):
                acc = acc + raw_ref[r0 + first + k:r0 + first + k + FEAT_ROWS, lanes] * cw_ref[k:k + 1, lanes]
            acc = acc + cb_ref[:, lanes]
            xbc_ref[r0:r0 + FEAT_ROWS, lanes] = (acc * _sigmoid(acc)).astype(BF16)
        return carry

    lax.fori_loop(0, n_lane_blk, body, 0)

    dtr = _dot(h_ref[...], wdt_ref[...]) + bdt_ref[...]
    sp = jnp.maximum(dtr, 0.0) + jnp.log1p(jnp.exp(-jnp.abs(dtr)))
    lane = lax.broadcasted_iota(jnp.int32, (1, LANES), 1)
    dt_ref[...] = jnp.where(lane < N_DIR * N_HEADS, sp, 0.0)


def _ssd_features(h, wx, bx, wdt, bdt, cw, cb, *, tm, lseq):
    t = h.shape[0]
    nb = tm // BF16_ROWS
    last = t // BF16_ROWS - 1
    return pl.pallas_call(
        functools.partial(_feat_kernel, lseq=lseq),
        grid=(t // tm,),
        in_specs=[pl.BlockSpec((BF16_ROWS, D_MODEL), lambda i: (jnp.maximum(i * nb - 1, 0), 0)),
                  pl.BlockSpec((tm, D_MODEL), lambda i: (i, 0)),
                  pl.BlockSpec((BF16_ROWS, D_MODEL), lambda i: (jnp.minimum((i + 1) * nb, last), 0)),
                  _const_spec((D_MODEL, XBC_DIM)), _const_spec((1, XBC_DIM)),
                  _const_spec((D_MODEL, LANES)), _const_spec((1, LANES)),
                  _const_spec((SSD_CONV_K, XBC_DIM)), _const_spec((1, XBC_DIM))],
        out_specs=[pl.BlockSpec((tm, XBC_DIM), lambda i: (i, 0)),
                   pl.BlockSpec((tm, LANES), lambda i: (i, 0))],
        out_shape=[jax.ShapeDtypeStruct((t, XBC_DIM), BF16), jax.ShapeDtypeStruct((t, LANES), F32)],
        scratch_shapes=[pltpu.VMEM((tm + 2 * BF16_ROWS, XBC_DIM), F32)],
        compiler_params=_params(), name="ssd_features",
    )(h, h, h, wx, bx.reshape(1, -1), wdt, bdt.reshape(1, -1), cw, cb.reshape(1, -1))


def _scan_kernel(xs_ref, bm_ref, cm_ref, dt_ref, a_ref, s0_ref, y_ref, sf_ref, state_ref, *, reverse, d):
    tb = xs_ref.shape[0]
    nch = tb // CHUNK
    j = pl.program_id(1)

    @pl.when(j == 0)
    def _():
        state_ref[...] = s0_ref[0]

    row_i = lax.broadcasted_iota(jnp.int32, (CHUNK, CHUNK), 0)
    col_i = lax.broadcasted_iota(jnp.int32, (CHUNK, CHUNK), 1)
    causal = (col_i >= row_i) if reverse else (row_i >= col_i)
    tri = causal.astype(F32)
    edge = 0 if reverse else CHUNK - 1
    lo64 = lax.broadcasted_iota(jnp.int32, (1, LANES), 1) < HEAD_DIM
    head_of_lane = lax.broadcasted_iota(jnp.int32, (1, GROUP_W), 1) // HEAD_DIM

    def pair(v0, v1):
        return jnp.where(lo64, v0, v1)

    def chunk_body(c, carry):
        ci = (nch - 1 - c) if reverse else c
        r0 = pl.multiple_of(ci * CHUNK, CHUNK)
        rows = pl.ds(r0, CHUNK)
        dt = dt_ref[rows, :]
        acum = jnp.dot(tri, dt * a_ref[...], precision=lax.Precision.HIGHEST,
                       preferred_element_type=F32)
        acum_t = acum.T
        dt_t = dt.T
        a_edge = acum[edge:edge + 1, :]
        w_all = jnp.exp(a_edge - acum) * dt
        for g in range(N_GROUPS):
            bg = bm_ref[rows, g * D_STATE:(g + 1) * D_STATE]
            cg = cm_ref[rows, g * D_STATE:(g + 1) * D_STATE]
            xg = xs_ref[rows, g * GROUP_W:(g + 1) * GROUP_W]
            bg_t = bg.astype(F32).T.astype(BF16)
            cb = _dot(cg, bg_t)
            s_in = state_ref[g]
            y_off = _dot(cg, s_in.astype(BF16))
            m_parts, e_cols, w_cols = [], [], []
            for hh in range(HEADS_PER_GROUP):
                q = d * N_HEADS + g * HEADS_PER_GROUP + hh
                a_col = jnp.broadcast_to(acum[:, q:q + 1], (CHUNK, CHUNK))
                seg = jnp.where(causal, a_col - acum_t[q:q + 1, :], -1e30)
                m_parts.append((cb * jnp.exp(seg) * dt_t[q:q + 1, :]).astype(BF16))
                e_cols.append(jnp.exp(a_col))
                w_cols.append(jnp.broadcast_to(w_all[:, q:q + 1], (CHUNK, CHUNK)))
            lhs = jnp.concatenate(m_parts, axis=1)
            rhs = jnp.concatenate([jnp.where(head_of_lane == hh, xg, jnp.zeros_like(xg))
                                   for hh in range(HEADS_PER_GROUP)], axis=0)
            e_grp = jnp.concatenate([pair(e_cols[0], e_cols[1]), pair(e_cols[2], e_cols[3])], axis=1)
            w_grp = jnp.concatenate([pair(w_cols[0], w_cols[1]), pair(w_cols[2], w_cols[3])], axis=1)
            y = _dot(lhs, rhs) + y_off * e_grp
            y_ref[rows, g * GROUP_W:(g + 1) * GROUP_W] = y.astype(BF16)
            xw = (xg.astype(F32) * w_grp).astype(BF16)
            state_ref[g] = s_in * e_grp[edge:edge + 1, :] + _dot(bg_t, xw)
        return carry

    lax.fori_loop(0, nch, chunk_body, 0)

    @pl.when(j == pl.num_programs(1) - 1)
    def _():
        sf_ref[0] = state_ref[...]


def _ssd_scan(xbc, dt, a128, s0, *, bsz, lseq, tb, reverse, d):
    nblk = lseq // tb

    def blk(b, j):
        return b * nblk + ((nblk - 1 - j) if reverse else j)

    n_xs = D_INNER // (N_GROUPS * D_STATE)
    st_spec = pl.BlockSpec((1, N_GROUPS, D_STATE, GROUP_W), lambda b, j: (b, 0, 0, 0))
    return pl.pallas_call(
        functools.partial(_scan_kernel, reverse=reverse, d=d),
        grid=(bsz, nblk),
        in_specs=[pl.BlockSpec((tb, D_INNER), lambda b, j: (blk(b, j), 0)),
                  pl.BlockSpec((tb, N_GROUPS * D_STATE), lambda b, j: (blk(b, j), n_xs)),
                  pl.BlockSpec((tb, N_GROUPS * D_STATE), lambda b, j: (blk(b, j), n_xs + 1)),
                  pl.BlockSpec((tb, LANES), lambda b, j: (blk(b, j), 0)),
                  pl.BlockSpec((1, LANES), lambda b, j: (0, 0)),
                  st_spec],
        out_specs=[pl.BlockSpec((tb, D_INNER), lambda b, j: (blk(b, j), 0)), st_spec],
        out_shape=[jax.ShapeDtypeStruct((bsz * lseq, D_INNER), BF16),
                   jax.ShapeDtypeStruct((bsz, N_GROUPS, D_STATE, GROUP_W), F32)],
        scratch_shapes=[pltpu.VMEM((N_GROUPS, D_STATE, GROUP_W), F32)],
        compiler_params=_params(2), name="ssd_scan_bwd" if reverse else "ssd_scan_fwd",
    )(xbc, xbc, xbc, dt, a128, s0)


def _merge_kernel(x_ref, h_ref, yf_ref, yb_ref, xs_ref, yc_ref, mod_ref, wz_ref, bz_ref, wg_ref, bg_ref,
                  dsk_ref, nrm_ref, wso_ref, wo_ref, o_ref, *, k_gate):
    hb = h_ref[...]
    z = _dot(hb, wz_ref[...]) + bz_ref[...]
    y = yf_ref[...].astype(F32) + yb_ref[...].astype(F32) + dsk_ref[...] * xs_ref[...].astype(F32)
    y = y * (z * _sigmoid(z))
    parts = []
    for g in range(N_GROUPS):
        yg = y[:, g * GROUP_W:(g + 1) * GROUP_W]
        ms = jnp.mean(yg * yg, axis=-1, keepdims=True)
        parts.append(yg * lax.rsqrt(ms + EPS))
    yn = jnp.concatenate(parts, axis=1) * nrm_ref[...]
    y_s = _dot(yn.astype(BF16), wso_ref[...])
    graw = _dot(hb, wg_ref[...]) + bg_ref[...]
    m = _sigmoid(graw[:, :D_MODEL]) * yc_ref[...].astype(F32) + _sigmoid(graw[:, D_MODEL:]) * y_s
    mix = _dot(m.astype(BF16), wo_ref[...])
    o_ref[...] = x_ref[...] + mod_ref[0, k_gate:k_gate + 1, :] * mix


def _merge(x, h, yf, yb, xbc, yc, mods, mod_map, wz, bz, wg, bg, dsk, nrm, wso, wo, *, tm, k_gate):
    t = x.shape[0]
    row = pl.BlockSpec((tm, D_MODEL), lambda i: (i, 0))
    wide = pl.BlockSpec((tm, D_INNER), lambda i: (i, 0))
    return pl.pallas_call(
        functools.partial(_merge_kernel, k_gate=k_gate),
        grid=(t // tm,),
        in_specs=[row, row, wide, wide, wide, row,
                  pl.BlockSpec((1, N_MOD, D_MODEL), mod_map),
                  _const_spec((D_MODEL, D_INNER)), _const_spec((1, D_INNER)),
                  _const_spec((D_MODEL, 2 * D_MODEL)), _const_spec((1, 2 * D_MODEL)),
                  _const_spec((1, D_INNER)), _const_spec((1, D_INNER)),
                  _const_spec((D_INNER, D_MODEL)), _const_spec((D_MODEL, D_MODEL))],
        out_specs=row,
        out_shape=jax.ShapeDtypeStruct((t, D_MODEL), F32),
        compiler_params=_params(), name="mixer_merge",
    )(x, h, yf, yb, xbc, yc, mods, wz, bz.reshape(1, -1), wg, bg.reshape(1, -1), dsk.reshape(1, -1),
      nrm.reshape(1, -1), wso, wo)


def kernel(x, c, ctx, c_ctx, w_ada, b_ada, ffn1_norm, ffn1_w_gate, ffn1_w_up, ffn1_w_down, mix_norm, w_in, b_in, conv_dw_w, conv_dw_b, conv_ln_g, conv_ln_b, conv_w_out, ssd_conv_w, ssd_conv_b, ssd_dt_bias, ssd_a_log, ssd_d, ssd_norm, ssd_w_out, w_out, ffn2_norm, ffn2_w_gate, ffn2_w_up, ffn2_w_down, final_norm):
    bsz, seq, _ = x.shape
    ctx_len = ctx.shape[1]
    depth = w_ada.shape[0]
    assert seq % 512 == 0 and ctx_len % 256 == 0 and bsz + 1 <= 8

    tm_x, tm_c = 512, 256
    xf = x.reshape(bsz * seq, D_MODEL)
    cf = ctx.reshape(bsz * ctx_len, D_MODEL)
    x_map = lambda i: ((i * tm_x) // seq, 0, 0)
    x_map_half = lambda i: ((i * (tm_x // 2)) // seq, 0, 0)
    c_map = lambda i: (bsz, 0, 0)

    cvec = jnp.zeros((8, D_MODEL), F32).at[:bsz].set(c).at[bsz].set(c_ctx)
    mods_all = _mods(cvec, w_ada, b_ada)[:, :bsz + 1].reshape(depth, bsz + 1, N_MOD, D_MODEL)

    s1 = 2 * CONV_DIM
    s2 = s1 + D_INNER
    s3 = s2 + XBC_DIM
    s4 = s3 + N_DIR * N_HEADS
    zero_state = jnp.zeros((bsz, N_GROUPS, D_STATE, GROUP_W), F32)

    for i in range(depth):
        last = i == depth - 1
        mods = mods_all[i]
        bf = lambda w: w.astype(BF16)
        w1 = (bf(ffn1_w_gate[i]), bf(ffn1_w_up[i]), bf(ffn1_w_down[i]))
        w2 = (bf(ffn2_w_gate[i]), bf(ffn2_w_up[i]), bf(ffn2_w_down[i]))
        wi, bi = w_in[i], b_in[i]
        wa, wb = bf(wi[:, :CONV_DIM]), bf(wi[:, CONV_DIM:s1])
        ba, bb = bi[:CONV_DIM], bi[CONV_DIM:s1]
        wz, bz = bf(wi[:, s1:s2]), bi[s1:s2]
        wx, bx = bf(wi[:, s2:s3]), bi[s2:s3]
        n_dt = N_DIR * N_HEADS
        wdt = bf(jnp.zeros((D_MODEL, LANES), F32).at[:, :n_dt].set(wi[:, s3:s4]))
        pad_dt = lambda v: jnp.zeros((LANES,), F32).at[:n_dt].set(v.reshape(-1))
        bdt = pad_dt(bi[s3:s4] + ssd_dt_bias[i].reshape(-1))
        a128 = pad_dt(-jnp.exp(ssd_a_log[i].astype(F32))).reshape(1, LANES)
        wgt, bgt = bf(wi[:, s4:]), bi[s4:]
        dsk = jnp.repeat(ssd_d[i], HEAD_DIM)
        conv_w = (wa, wb, ba, bb, conv_dw_w[i], conv_dw_b[i], conv_ln_g[i], conv_ln_b[i], bf(conv_w_out[i]))
        merge_w = (wz, bz, wgt, bgt, dsk, ssd_norm[i], bf(ssd_w_out[i]), bf(w_out[i]))
        feat_w = (wx, bx, wdt, bdt, ssd_conv_w[i], ssd_conv_b[i])

        xf, hx = _ffn(xf, mods, x_map, ffn1_norm[i], *w1, tm=tm_x, k0=0, post="mod", g2=mix_norm[i], k1=3)
        cf, hc = _ffn(cf, mods, c_map, ffn1_norm[i], *w1, tm=tm_c, k0=0, post="mod", g2=mix_norm[i], k1=3)

        xbc_c, dt_c = _ssd_features(hc, *feat_w, tm=tm_c, lseq=ctx_len)
        xbc_x, dt_x = _ssd_features(hx, *feat_w, tm=tm_x, lseq=seq)
        scan_c = functools.partial(_ssd_scan, xbc_c, dt_c, a128, bsz=bsz, lseq=ctx_len, tb=256)
        scan_x = functools.partial(_ssd_scan, xbc_x, dt_x, a128, bsz=bsz, lseq=seq, tb=512)
        yf_c, s_f = scan_c(zero_state, reverse=False, d=0)
        yb_c, s_b = scan_c(zero_state, reverse=True, d=1)
        yf_x, _ = scan_x(s_f, reverse=False, d=0)
        yb_x, _ = scan_x(s_b, reverse=True, d=1)

        yc_x = _conv_branch(hx, *conv_w, tm=tm_x // 2, seg=GRID_W)
        xf = _merge(xf, hx, yf_x, yb_x, xbc_x, yc_x, mods, x_map_half, *merge_w, tm=tm_x // 2, k_gate=5)
        if not last:
            yc_c = _conv_branch(hc, *conv_w, tm=tm_c, seg=ctx_len)
            cf = _merge(cf, hc, yf_c, yb_c, xbc_c, yc_c, mods, c_map, *merge_w, tm=tm_c, k_gate=5)

        if last:
            xf = _ffn(xf, mods, x_map, ffn2_norm[i], *w2, tm=tm_x, k0=6, post="final", g2=final_norm)
        else:
            xf = _ffn(xf, mods, x_map, ffn2_norm[i], *w2, tm=tm_x, k0=6)
            cf = _ffn(cf, mods, c_map, ffn2_norm[i], *w2, tm=tm_c, k0=6)
    return xf.reshape(bsz, seq, D_MODEL)
```

```python
import functools

import jax
import jax.numpy as jnp
import numpy as np
from jax import lax
from jax.experimental import pallas as pl
from jax.experimental.pallas import tpu as pltpu

F32 = jnp.float32
BF16 = jnp.bfloat16

D_MODEL = 1024
N_MOD = 9
D_FF = 2816
CONV_DIM = D_MODEL
CONV_K = 31
D_INNER = 2 * D_MODEL
HEAD_DIM = 64
N_HEADS = D_INNER // HEAD_DIM
N_GROUPS = 8
HEADS_PER_GROUP = N_HEADS // N_GROUPS
GROUP_W = HEADS_PER_GROUP * HEAD_DIM
D_STATE = 128
SSD_CONV_K = 5
CHUNK = 128
N_DIR = 2
XBC_DIM = D_INNER + 2 * N_GROUPS * D_STATE
GRID_W = 64
EPS = 1e-6

LANES = 128
BF16_ROWS = 16
FF_CHUNK = 256
VMEM_LIMIT = 58 * 1024 * 1024


def _dot(a, b):
    return jnp.dot(a, b, preferred_element_type=F32)


def _sigmoid(v):
    return 1.0 / (1.0 + jnp.exp(-v))


def _const_spec(shape):
    nd = len(shape)
    return pl.BlockSpec(shape, lambda *_: (0,) * nd, pipeline_mode=pl.Buffered(1))


def _params(n_axes=1):
    return pltpu.CompilerParams(dimension_semantics=("arbitrary",) * n_axes,
                                vmem_limit_bytes=VMEM_LIMIT)


def _mods_kernel(c_ref, w_ref, b_ref, o_ref):
    c = c_ref[...]
    act = c * _sigmoid(c)
    o_ref[0] = jnp.dot(act, w_ref[0], precision=lax.Precision.HIGHEST,
                       preferred_element_type=F32) + b_ref[0]


def _mods(cvec, w_ada, b_ada):
    depth, _, n_out = w_ada.shape
    tn = D_MODEL
    return pl.pallas_call(
        _mods_kernel,
        grid=(depth, n_out // tn),
        in_specs=[pl.BlockSpec((8, D_MODEL), lambda l, j: (0, 0)),
                  pl.BlockSpec((1, D_MODEL, tn), lambda l, j: (l, 0, j)),
                  pl.BlockSpec((1, 1, tn), lambda l, j: (l, 0, j))],
        out_specs=pl.BlockSpec((1, 8, tn), lambda l, j: (l, 0, j)),
        out_shape=jax.ShapeDtypeStruct((depth, 8, n_out), F32),
        compiler_params=_params(2),
        name="ada_mods",
    )(cvec, w_ada, b_ada.reshape(depth, 1, n_out))


def _rms(v, g):
    ms = jnp.mean(v * v, axis=-1, keepdims=True)
    return v * lax.rsqrt(ms + EPS) * g


def _ffn_kernel(*refs, k0, post, k1):
    if post == "mod":
        x_ref, mod_ref, g_ref, wg_ref, wu_ref, wd_ref, g2_ref, o_ref, h_ref = refs
    elif post == "final":
        x_ref, mod_ref, g_ref, wg_ref, wu_ref, wd_ref, g2_ref, o_ref = refs
    else:
        x_ref, mod_ref, g_ref, wg_ref, wu_ref, wd_ref, o_ref = refs
    x = x_ref[...]
    shift = mod_ref[0, k0:k0 + 1, :]
    scale = mod_ref[0, k0 + 1:k0 + 2, :]
    gate = mod_ref[0, k0 + 2:k0 + 3, :]
    hb = (_rms(x, g_ref[...]) * (1.0 + scale) + shift).astype(BF16)
    acc = jnp.zeros(x.shape, F32)
    for c in range(D_FF // FF_CHUNK):
        sl = slice(c * FF_CHUNK, (c + 1) * FF_CHUNK)
        g = _dot(hb, wg_ref[:, sl])
        u = _dot(hb, wu_ref[:, sl])
        a = (g * _sigmoid(g)) * u
        acc = acc + _dot(a.astype(BF16), wd_ref[sl, :])
    y = x + 0.5 * gate * acc
    if post == "final":
        o_ref[...] = _rms(y, g2_ref[...])
        return
    o_ref[...] = y
    if post == "mod":
        shift2 = mod_ref[0, k1:k1 + 1, :]
        scale2 = mod_ref[0, k1 + 1:k1 + 2, :]
        h_ref[...] = (_rms(y, g2_ref[...]) * (1.0 + scale2) + shift2).astype(BF16)


def _ffn(x, mods, mod_map, norm_g, wg, wu, wd, *, tm, k0, post="none", g2=None, k1=0):
    t = x.shape[0]
    row = pl.BlockSpec((tm, D_MODEL), lambda i: (i, 0))
    in_specs = [row,
                pl.BlockSpec((1, N_MOD, D_MODEL), mod_map),
                _const_spec((1, D_MODEL)),
                _const_spec((D_MODEL, D_FF)), _const_spec((D_MODEL, D_FF)),
                _const_spec((D_FF, D_MODEL))]
    args = [x, mods, norm_g.reshape(1, D_MODEL), wg, wu, wd]
    if post != "none":
        in_specs.append(_const_spec((1, D_MODEL)))
        args.append(g2.reshape(1, D_MODEL))
    if post == "mod":
        out_specs = [row, row]
        out_shape = [jax.ShapeDtypeStruct((t, D_MODEL), F32), jax.ShapeDtypeStruct((t, D_MODEL), BF16)]
    else:
        out_specs = row
        out_shape = jax.ShapeDtypeStruct((t, D_MODEL), F32)
    return pl.pallas_call(
        functools.partial(_ffn_kernel, k0=k0, post=post, k1=k1),
        grid=(t // tm,), in_specs=in_specs, out_specs=out_specs, out_shape=out_shape,
        compiler_params=_params(), name="half_ffn_" + post,
    )(*args)


CONV_K_PAD = 32


def _split_bf16(m):
    hi = jnp.asarray(m, F32).astype(BF16)
    lo = (jnp.asarray(m, F32) - hi.astype(F32)).astype(BF16)
    return hi, lo


def _dft_consts(seg):
    n = 2 * seg
    t = np.arange(seg)
    r = np.arange(seg)
    th = 2.0 * np.pi * np.outer(r, t) / n
    fwd = np.concatenate([np.cos(th), np.sin(th)], axis=0)
    fwd[seg] = np.cos(np.pi * t)
    inv = np.concatenate([2.0 * np.cos(th.T), 2.0 * np.sin(th.T)], axis=1) / n
    inv[:, 0] = 1.0 / n
    inv[:, seg] = np.cos(np.pi * t) / n
    shift = CONV_K // 2 - np.arange(CONV_K)
    ph = 2.0 * np.pi * np.outer(r, shift) / n
    re, im = np.cos(ph), -np.sin(ph)
    re_nyq = re.copy()
    re_nyq[0] = np.cos(np.pi * shift)
    taps = np.zeros((3 * seg, CONV_K_PAD))
    taps[:, :CONV_K] = np.concatenate([re, im, re_nyq], axis=0)
    f2 = jnp.concatenate(_split_bf16(fwd), axis=1)
    g2 = jnp.concatenate(_split_bf16(inv), axis=1)
    return f2, g2, jnp.asarray(taps, F32)


def _conv_kernel(h_ref, wa_ref, wb_ref, ba_ref, bb_ref, dww_ref, dwb_ref, lng_ref, lnb_ref, wo_ref,
                 f2_ref, g2_ref, taps_ref, o_ref, coef_ref, conv_ref, *, seg):
    tm = h_ref.shape[0]

    @pl.when(pl.program_id(0) == 0)
    def _():
        coef_ref[...] = jnp.dot(taps_ref[...], dww_ref[...], precision=lax.Precision.HIGHEST,
                                preferred_element_type=F32)

    hb = h_ref[...]
    a = _dot(hb, wa_ref[...]) + ba_ref[...]
    b = _dot(hb, wb_ref[...]) + bb_ref[...]
    ub = (a * _sigmoid(b)).astype(BF16)
    h_re = coef_ref[0:seg, :]
    h_im = coef_ref[seg:2 * seg, :]
    h_re_nyq = coef_ref[2 * seg:3 * seg, :]
    for s in range(tm // seg):
        us = ub[s * seg:(s + 1) * seg, :]
        spec = _dot(f2_ref[...], jnp.concatenate([us, us], axis=0))
        co, si = spec[:seg, :], spec[seg:, :]
        y = jnp.concatenate([co * h_re + si * h_im, si * h_re_nyq - co * h_im], axis=0).astype(BF16)
        conv_ref[s * seg:(s + 1) * seg, :] = _dot(g2_ref[...], jnp.concatenate([y, y], axis=0)) + dwb_ref[...]

    v = conv_ref[...]
    mu = jnp.mean(v, axis=-1, keepdims=True)
    vc = v - mu
    var = jnp.mean(vc * vc, axis=-1, keepdims=True)
    vn = vc * lax.rsqrt(var + EPS) * lng_ref[...] + lnb_ref[...]
    act = vn * _sigmoid(vn)
    o_ref[...] = _dot(act.astype(BF16), wo_ref[...]).astype(BF16)


def _conv_branch(h, wa, wb, ba, bb, dww, dwb, lng, lnb, wo, *, tm, seg):
    t = h.shape[0]
    assert 2 * seg >= seg + CONV_K - 1
    f2, g2, taps = _dft_consts(seg)
    dww_pad = jnp.zeros((CONV_K_PAD, CONV_DIM), F32).at[:CONV_K].set(dww)
    row = pl.BlockSpec((tm, D_MODEL), lambda i: (i, 0))
    vec = _const_spec((1, CONV_DIM))
    sq = _const_spec((D_MODEL, CONV_DIM))
    return pl.pallas_call(
        functools.partial(_conv_kernel, seg=seg),
        grid=(t // tm,),
        in_specs=[row, sq, sq, vec, vec, _const_spec((CONV_K_PAD, CONV_DIM)), vec, vec, vec, sq,
                  _const_spec(f2.shape), _const_spec(g2.shape), _const_spec(taps.shape)],
        out_specs=row,
        out_shape=jax.ShapeDtypeStruct((t, D_MODEL), BF16),
        scratch_shapes=[pltpu.VMEM((3 * seg, CONV_DIM), F32),
                        pltpu.VMEM((tm, CONV_DIM), F32)],
        compiler_params=_params(), name="conv_branch",
    )(h, wa, wb, ba.reshape(1, -1), bb.reshape(1, -1), dww_pad, dwb.reshape(1, -1), lng.reshape(1, -1),
      lnb.reshape(1, -1), wo, f2, g2, taps)


FEAT_ROWS = 64


def _feat_kernel(hp_ref, h_ref, hn_ref, wx_ref, bx_ref, wdt_ref, bdt_ref, cw_ref, cb_ref,
                 xbc_ref, dt_ref, raw_ref, *, lseq):
    tm = h_ref.shape[0]
    halo = BF16_ROWS
    i = pl.program_id(0)
    hfull = jnp.concatenate([hp_ref[...], h_ref[...], hn_ref[...]], axis=0)
    raw = _dot(hfull, wx_ref[...]) + bx_ref[...]
    t0 = i * tm
    pos = t0 - halo + lax.broadcasted_iota(jnp.int32, (tm + 2 * halo, 1), 0)
    seq_lo = (t0 // lseq) * lseq
    valid = (pos >= seq_lo) & (pos < seq_lo + lseq)
    raw_ref[...] = jnp.where(valid, raw, 0.0)

    n_lane_blk = XBC_DIM // LANES
    first = halo - SSD_CONV_K // 2

    def body(j, carry):
        cl = pl.multiple_of(j * LANES, LANES)
        lanes = pl.ds(cl, LANES)
        for r0 in range(0, tm, FEAT_ROWS):
            acc = jnp.zeros((FEAT_ROWS, LANES), F32)
            for k in range(SSD_CONV_K):
                acc = acc + raw_ref[r0 + first + k:r0 + first + k + FEAT_ROWS, lanes] * cw_ref[k:k + 1, lanes]
            acc = acc + cb_ref[:, lanes]
            xbc_ref[r0:r0 + FEAT_ROWS, lanes] = (acc * _sigmoid(acc)).astype(BF16)
        return carry

    lax.fori_loop(0, n_lane_blk, body, 0)

    dtr = _dot(h_ref[...], wdt_ref[...]) + bdt_ref[...]
    sp = jnp.maximum(dtr, 0.0) + jnp.log1p(jnp.exp(-jnp.abs(dtr)))
    lane = lax.broadcasted_iota(jnp.int32, (1, LANES), 1)
    dt_ref[...] = jnp.where(lane < N_DIR * N_HEADS, sp, 0.0)


def _ssd_features(h, wx, bx, wdt, bdt, cw, cb, *, tm, lseq):
    t = h.shape[0]
    nb = tm // BF16_ROWS
    last = t // BF16_ROWS - 1
    return pl.pallas_call(
        functools.partial(_feat_kernel, lseq=lseq),
        grid=(t // tm,),
        in_specs=[pl.BlockSpec((BF16_ROWS, D_MODEL), lambda i: (jnp.maximum(i * nb - 1, 0), 0)),
                  pl.BlockSpec((tm, D_MODEL), lambda i: (i, 0)),
                  pl.BlockSpec((BF16_ROWS, D_MODEL), lambda i: (jnp.minimum((i + 1) * nb, last), 0)),
                  _const_spec((D_MODEL, XBC_DIM)), _const_spec((1, XBC_DIM)),
                  _const_spec((D_MODEL, LANES)), _const_spec((1, LANES)),
                  _const_spec((SSD_CONV_K, XBC_DIM)), _const_spec((1, XBC_DIM))],
        out_specs=[pl.BlockSpec((tm, XBC_DIM), lambda i: (i, 0)),
                   pl.BlockSpec((tm, LANES), lambda i: (i, 0))],
        out_shape=[jax.ShapeDtypeStruct((t, XBC_DIM), BF16), jax.ShapeDtypeStruct((t, LANES), F32)],
        scratch_shapes=[pltpu.VMEM((tm + 2 * BF16_ROWS, XBC_DIM), F32)],
        compiler_params=_params(), name="ssd_features",
    )(h, h, h, wx, bx.reshape(1, -1), wdt, bdt.reshape(1, -1), cw, cb.reshape(1, -1))


def _scan_kernel(xs_ref, bm_ref, cm_ref, dt_ref, a_ref, s0_ref, y_ref, sf_ref, state_ref, *, reverse, d):
    tb = xs_ref.shape[0]
    nch = tb // CHUNK
    j = pl.program_id(1)

    @pl.when(j == 0)
    def _():
        state_ref[...] = s0_ref[0]

    row_i = lax.broadcasted_iota(jnp.int32, (CHUNK, CHUNK), 0)
    col_i = lax.broadcasted_iota(jnp.int32, (CHUNK, CHUNK), 1)
    causal = (col_i >= row_i) if reverse else (row_i >= col_i)
    tri = causal.astype(F32)
    edge = 0 if reverse else CHUNK - 1
    lo64 = lax.broadcasted_iota(jnp.int32, (1, LANES), 1) < HEAD_DIM
    head_of_lane = lax.broadcasted_iota(jnp.int32, (1, GROUP_W), 1) // HEAD_DIM

    def pair(v0, v1):
        return jnp.where(lo64, v0, v1)

    def chunk_body(c, carry):
        ci = (nch - 1 - c) if reverse else c
        r0 = pl.multiple_of(ci * CHUNK, CHUNK)
        rows = pl.ds(r0, CHUNK)
        dt = dt_ref[rows, :]
        acum = jnp.dot(tri, dt * a_ref[...], precision=lax.Precision.HIGHEST,
                       preferred_element_type=F32)
        acum_t = acum.T
        dt_t = dt.T
        a_edge = acum[edge:edge + 1, :]
        w_all = jnp.exp(a_edge - acum) * dt
        for g in range(N_GROUPS):
            bg = bm_ref[rows, g * D_STATE:(g + 1) * D_STATE]
            cg = cm_ref[rows, g * D_STATE:(g + 1) * D_STATE]
            xg = xs_ref[rows, g * GROUP_W:(g + 1) * GROUP_W]
            bg_t = bg.astype(F32).T.astype(BF16)
            cb = _dot(cg, bg_t)
            s_in = state_ref[g]
            y_off = _dot(cg, s_in.astype(BF16))
            m_parts, e_cols, w_cols = [], [], []
            for hh in range(HEADS_PER_GROUP):
                q = d * N_HEADS + g * HEADS_PER_GROUP + hh
                a_col = jnp.broadcast_to(acum[:, q:q + 1], (CHUNK, CHUNK))
                seg = jnp.where(causal, a_col - acum_t[q:q + 1, :], -1e30)
                m_parts.append((cb * jnp.exp(seg) * dt_t[q:q + 1, :]).astype(BF16))
                e_cols.append(jnp.exp(a_col))
                w_cols.append(jnp.broadcast_to(w_all[:, q:q + 1], (CHUNK, CHUNK)))
            lhs = jnp.concatenate(m_parts, axis=1)
            rhs = jnp.concatenate([jnp.where(head_of_lane == hh, xg, jnp.zeros_like(xg))
                                   for hh in range(HEADS_PER_GROUP)], axis=0)
            e_grp = jnp.concatenate([pair(e_cols[0], e_cols[1]), pair(e_cols[2], e_cols[3])], axis=1)
            w_grp = jnp.concatenate([pair(w_cols[0], w_cols[1]), pair(w_cols[2], w_cols[3])], axis=1)
            y = _dot(lhs, rhs) + y_off * e_grp
            y_ref[rows, g * GROUP_W:(g + 1) * GROUP_W] = y.astype(BF16)
            xw = (xg.astype(F32) * w_grp).astype(BF16)
            state_ref[g] = s_in * e_grp[edge:edge + 1, :] + _dot(bg_t, xw)
        return carry

    lax.fori_loop(0, nch, chunk_body, 0)

    @pl.when(j == pl.num_programs(1) - 1)
    def _():
        sf_ref[0] = state_ref[...]


def _ssd_scan(xbc, dt, a128, s0, *, bsz, lseq, tb, reverse, d):
    nblk = lseq // tb

    def blk(b, j):
        return b * nblk + ((nblk - 1 - j) if reverse else j)

    n_xs = D_INNER // (N_GROUPS * D_STATE)
    st_spec = pl.BlockSpec((1, N_GROUPS, D_STATE, GROUP_W), lambda b, j: (b, 0, 0, 0))
    return pl.pallas_call(
        functools.partial(_scan_kernel, reverse=reverse, d=d),
        grid=(bsz, nblk),
        in_specs=[pl.BlockSpec((tb, D_INNER), lambda b, j: (blk(b, j), 0)),
                  pl.BlockSpec((tb, N_GROUPS * D_STATE), lambda b, j: (blk(b, j), n_xs)),
                  pl.BlockSpec((tb, N_GROUPS * D_STATE), lambda b, j: (blk(b, j), n_xs + 1)),
                  pl.BlockSpec((tb, LANES), lambda b, j: (blk(b, j), 0)),
                  pl.BlockSpec((1, LANES), lambda b, j: (0, 0)),
                  st_spec],
        out_specs=[pl.BlockSpec((tb, D_INNER), lambda b, j: (blk(b, j), 0)), st_spec],
        out_shape=[jax.ShapeDtypeStruct((bsz * lseq, D_INNER), BF16),
                   jax.ShapeDtypeStruct((bsz, N_GROUPS, D_STATE, GROUP_W), F32)],
        scratch_shapes=[pltpu.VMEM((N_GROUPS, D_STATE, GROUP_W), F32)],
        compiler_params=_params(2), name="ssd_scan_bwd" if reverse else "ssd_scan_fwd",
    )(xbc, xbc, xbc, dt, a128, s0)


def _merge_kernel(x_ref, h_ref, yf_ref, yb_ref, xs_ref, yc_ref, mod_ref, wz_ref, bz_ref, wg_ref, bg_ref,
                  dsk_ref, nrm_ref, wso_ref, wo_ref, o_ref, *, k_gate):
    hb = h_ref[...]
    z = _dot(hb, wz_ref[...]) + bz_ref[...]
    y = yf_ref[...].astype(F32) + yb_ref[...].astype(F32) + dsk_ref[...] * xs_ref[...].astype(F32)
    y = y * (z * _sigmoid(z))
    parts = []
    for g in range(N_GROUPS):
        yg = y[:, g * GROUP_W:(g + 1) * GROUP_W]
        ms = jnp.mean(yg * yg, axis=-1, keepdims=True)
        parts.append(yg * lax.rsqrt(ms + EPS))
    yn = jnp.concatenate(parts, axis=1) * nrm_ref[...]
    y_s = _dot(yn.astype(BF16), wso_ref[...])
    graw = _dot(hb, wg_ref[...]) + bg_ref[...]
    m = _sigmoid(graw[:, :D_MODEL]) * yc_ref[...].astype(F32) + _sigmoid(graw[:, D_MODEL:]) * y_s
    mix = _dot(m.astype(BF16), wo_ref[...])
    o_ref[...] = x_ref[...] + mod_ref[0, k_gate:k_gate + 1, :] * mix


def _merge(x, h, yf, yb, xbc, yc, mods, mod_map, wz, bz, wg, bg, dsk, nrm, wso, wo, *, tm, k_gate):
    t = x.shape[0]
    row = pl.BlockSpec((tm, D_MODEL), lambda i: (i, 0))
    wide = pl.BlockSpec((tm, D_INNER), lambda i: (i, 0))
    return pl.pallas_call(
        functools.partial(_merge_kernel, k_gate=k_gate),
        grid=(t // tm,),
        in_specs=[row, row, wide, wide, wide, row,
                  pl.BlockSpec((1, N_MOD, D_MODEL), mod_map),
                  _const_spec((D_MODEL, D_INNER)), _const_spec((1, D_INNER)),
                  _const_spec((D_MODEL, 2 * D_MODEL)), _const_spec((1, 2 * D_MODEL)),
                  _const_spec((1, D_INNER)), _const_spec((1, D_INNER)),
                  _const_spec((D_INNER, D_MODEL)), _const_spec((D_MODEL, D_MODEL))],
        out_specs=row,
        out_shape=jax.ShapeDtypeStruct((t, D_MODEL), F32),
        compiler_params=_params(), name="mixer_merge",
    )(x, h, yf, yb, xbc, yc, mods, wz, bz.reshape(1, -1), wg, bg.reshape(1, -1), dsk.reshape(1, -1),
      nrm.reshape(1, -1), wso, wo)


def kernel(x, c, ctx, c_ctx, w_ada, b_ada, ffn1_norm, ffn1_w_gate, ffn1_w_up, ffn1_w_down, mix_norm, w_in, b_in, conv_dw_w, conv_dw_b, conv_ln_g, conv_ln_b, conv_w_out, ssd_conv_w, ssd_conv_b, ssd_dt_bias, ssd_a_log, ssd_d, ssd_norm, ssd_w_out, w_out, ffn2_norm, ffn2_w_gate, ffn2_w_up, ffn2_w_down, final_norm):
    bsz, seq, _ = x.shape
    ctx_len = ctx.shape[1]
    depth = w_ada.shape[0]
    assert seq % 512 == 0 and ctx_len % 256 == 0 and bsz + 1 <= 8

    tm_x, tm_c = 512, 256
    xf = x.reshape(bsz * seq, D_MODEL)
    cf = ctx.reshape(bsz * ctx_len, D_MODEL)
    x_map = lambda i: ((i * tm_x) // seq, 0, 0)
    x_map_half = lambda i: ((i * (tm_x // 2)) // seq, 0, 0)
    c_map = lambda i: (bsz, 0, 0)

    cvec = jnp.zeros((8, D_MODEL), F32).at[:bsz].set(c).at[bsz].set(c_ctx)
    mods_all = _mods(cvec, w_ada, b_ada)[:, :bsz + 1].reshape(depth, bsz + 1, N_MOD, D_MODEL)

    s1 = 2 * CONV_DIM
    s2 = s1 + D_INNER
    s3 = s2 + XBC_DIM
    s4 = s3 + N_DIR * N_HEADS
    zero_state = jnp.zeros((bsz, N_GROUPS, D_STATE, GROUP_W), F32)

    for i in range(depth):
        last = i == depth - 1
        mods = mods_all[i]
        bf = lambda w: w.astype(BF16)
        w1 = (bf(ffn1_w_gate[i]), bf(ffn1_w_up[i]), bf(ffn1_w_down[i]))
        w2 = (bf(ffn2_w_gate[i]), bf(ffn2_w_up[i]), bf(ffn2_w_down[i]))
        wi, bi = w_in[i], b_in[i]
        wa, wb = bf(wi[:, :CONV_DIM]), bf(wi[:, CONV_DIM:s1])
        ba, bb = bi[:CONV_DIM], bi[CONV_DIM:s1]
        wz, bz = bf(wi[:, s1:s2]), bi[s1:s2]
        wx, bx = bf(wi[:, s2:s3]), bi[s2:s3]
        n_dt = N_DIR * N_HEADS
        wdt = bf(jnp.zeros((D_MODEL, LANES), F32).at[:, :n_dt].set(wi[:, s3:s4]))
        pad_dt = lambda v: jnp.zeros((LANES,), F32).at[:n_dt].set(v.reshape(-1))
        bdt = pad_dt(bi[s3:s4] + ssd_dt_bias[i].reshape(-1))
        a128 = pad_dt(-jnp.exp(ssd_a_log[i].astype(F32))).reshape(1, LANES)
        wgt, bgt = bf(wi[:, s4:]), bi[s4:]
        dsk = jnp.repeat(ssd_d[i], HEAD_DIM)
        conv_w = (wa, wb, ba, bb, conv_dw_w[i], conv_dw_b[i], conv_ln_g[i], conv_ln_b[i], bf(conv_w_out[i]))
        merge_w = (wz, bz, wgt, bgt, dsk, ssd_norm[i], bf(ssd_w_out[i]), bf(w_out[i]))
        feat_w = (wx, bx, wdt, bdt, ssd_conv_w[i], ssd_conv_b[i])

        xf, hx = _ffn(xf, mods, x_map, ffn1_norm[i], *w1, tm=tm_x, k0=0, post="mod", g2=mix_norm[i], k1=3)
        cf, hc = _ffn(cf, mods, c_map, ffn1_norm[i], *w1, tm=tm_c, k0=0, post="mod", g2=mix_norm[i], k1=3)

        xbc_c, dt_c = _ssd_features(hc, *feat_w, tm=tm_c, lseq=ctx_len)
        xbc_x, dt_x = _ssd_features(hx, *feat_w, tm=tm_x, lseq=seq)
        scan_c = functools.partial(_ssd_scan, xbc_c, dt_c, a128, bsz=bsz, lseq=ctx_len, tb=256)
        scan_x = functools.partial(_ssd_scan, xbc_x, dt_x, a128, bsz=bsz, lseq=seq, tb=512)
        yf_c, s_f = scan_c(zero_state, reverse=False, d=0)
        yb_c, s_b = scan_c(zero_state, reverse=True, d=1)
        yf_x, _ = scan_x(s_f, reverse=False, d=0)
        yb_x, _ = scan_x(s_b, reverse=True, d=1)

        yc_x = _conv_branch(hx, *conv_w, tm=tm_x // 2, seg=GRID_W)
        xf = _merge(xf, hx, yf_x, yb_x, xbc_x, yc_x, mods, x_map_half, *merge_w, tm=tm_x // 2, k_gate=5)
        if not last:
            yc_c = _conv_branch(hc, *conv_w, tm=tm_c, seg=ctx_len)
            cf = _merge(cf, hc, yf_c, yb_c, xbc_c, yc_c, mods, c_map, *merge_w, tm=tm_c, k_gate=5)

        if last:
            xf = _ffn(xf, mods, x_map, ffn2_norm[i], *w2, tm=tm_x, k0=6, post="final", g2=final_norm)
        else:
            xf = _ffn(xf, mods, x_map, ffn2_norm[i], *w2, tm=tm_x, k0=6)
            cf = _ffn(cf, mods, c_map, ffn2_norm[i], *w2, tm=tm_c, k0=6)
    return xf.reshape(bsz, seq, D_MODEL)
```

```python
import functools

import jax
import jax.numpy as jnp
import numpy as np
from jax import lax
from jax.experimental import pallas as pl
from jax.experimental.pallas import tpu as pltpu

F32 = jnp.float32
BF16 = jnp.bfloat16

D_MODEL = 1024
N_MOD = 9
D_FF = 2816
CONV_DIM = D_MODEL
CONV_K = 31
D_INNER = 2 * D_MODEL
HEAD_DIM = 64
N_HEADS = D_INNER // HEAD_DIM
N_GROUPS = 8
HEADS_PER_GROUP = N_HEADS // N_GROUPS
GROUP_W = HEADS_PER_GROUP * HEAD_DIM
D_STATE = 128
SSD_CONV_K = 5
CHUNK = 128
N_DIR = 2
XBC_DIM = D_INNER + 2 * N_GROUPS * D_STATE
GRID_W = 64
EPS = 1e-6

LANES = 128
BF16_ROWS = 16
FF_CHUNK = 256
VMEM_LIMIT = 58 * 1024 * 1024


def _dot(a, b):
    return jnp.dot(a, b, preferred_element_type=F32)


LOG2E = 1.4426950408889634


def _sigmoid(v):
    return 1.0 / (1.0 + jnp.exp2(v * (-LOG2E)))


def _const_spec(shape):
    nd = len(shape)
    return pl.BlockSpec(shape, lambda *_: (0,) * nd, pipeline_mode=pl.Buffered(1))


def _params(n_axes=1, flags=None):
    return pltpu.CompilerParams(dimension_semantics=("arbitrary",) * n_axes,
                                vmem_limit_bytes=VMEM_LIMIT, flags=flags)


def _mods_kernel(c_ref, w_ref, b_ref, o_ref):
    c = c_ref[...]
    act = c * _sigmoid(c)
    o_ref[0] = jnp.dot(act, w_ref[0], precision=lax.Precision.HIGHEST,
                       preferred_element_type=F32) + b_ref[0]


def _mods(cvec, w_ada, b_ada):
    depth, _, n_out = w_ada.shape
    tn = D_MODEL
    return pl.pallas_call(
        _mods_kernel,
        grid=(depth, n_out // tn),
        in_specs=[pl.BlockSpec((8, D_MODEL), lambda l, j: (0, 0)),
                  pl.BlockSpec((1, D_MODEL, tn), lambda l, j: (l, 0, j)),
                  pl.BlockSpec((1, 1, tn), lambda l, j: (l, 0, j))],
        out_specs=pl.BlockSpec((1, 8, tn), lambda l, j: (l, 0, j)),
        out_shape=jax.ShapeDtypeStruct((depth, 8, n_out), F32),
        compiler_params=_params(2),
        name="ada_mods",
    )(cvec, w_ada, b_ada.reshape(depth, 1, n_out))


def _rms(v, g):
    ms = jnp.mean(v * v, axis=-1, keepdims=True)
    return v * lax.rsqrt(ms + EPS) * g


def _ffn_kernel(*refs, k0, post, k1):
    if post == "mod":
        x_ref, mod_ref, g_ref, wg_ref, wu_ref, wd_ref, g2_ref, o_ref, h_ref = refs
    elif post == "final":
        x_ref, mod_ref, g_ref, wg_ref, wu_ref, wd_ref, g2_ref, o_ref = refs
    else:
        x_ref, mod_ref, g_ref, wg_ref, wu_ref, wd_ref, o_ref = refs
    x = x_ref[...]
    shift = mod_ref[0, k0:k0 + 1, :]
    scale = mod_ref[0, k0 + 1:k0 + 2, :]
    gate = mod_ref[0, k0 + 2:k0 + 3, :]
    hb = (_rms(x, g_ref[...]) * (1.0 + scale) + shift).astype(BF16)
    acc = jnp.zeros(x.shape, F32)
    for c in range(D_FF // FF_CHUNK):
        sl = slice(c * FF_CHUNK, (c + 1) * FF_CHUNK)
        g = _dot(hb, wg_ref[:, sl])
        u = _dot(hb, wu_ref[:, sl])
        a = (g * _sigmoid(g)) * u
        acc = acc + _dot(a.astype(BF16), wd_ref[sl, :])
    y = x + 0.5 * gate * acc
    if post == "final":
        o_ref[...] = _rms(y, g2_ref[...])
        return
    o_ref[...] = y
    if post == "mod":
        shift2 = mod_ref[0, k1:k1 + 1, :]
        scale2 = mod_ref[0, k1 + 1:k1 + 2, :]
        h_ref[...] = (_rms(y, g2_ref[...]) * (1.0 + scale2) + shift2).astype(BF16)


def _ffn(x, mods, mod_map, norm_g, wg, wu, wd, *, tm, k0, post="none", g2=None, k1=0):
    t = x.shape[0]
    row = pl.BlockSpec((tm, D_MODEL), lambda i: (i, 0))
    in_specs = [row,
                pl.BlockSpec((1, N_MOD, D_MODEL), mod_map),
                _const_spec((1, D_MODEL)),
                _const_spec((D_MODEL, D_FF)), _const_spec((D_MODEL, D_FF)),
                _const_spec((D_FF, D_MODEL))]
    args = [x, mods, norm_g.reshape(1, D_MODEL), wg, wu, wd]
    if post != "none":
        in_specs.append(_const_spec((1, D_MODEL)))
        args.append(g2.reshape(1, D_MODEL))
    if post == "mod":
        out_specs = [row, row]
        out_shape = [jax.ShapeDtypeStruct((t, D_MODEL), F32), jax.ShapeDtypeStruct((t, D_MODEL), BF16)]
    else:
        out_specs = row
        out_shape = jax.ShapeDtypeStruct((t, D_MODEL), F32)
    return pl.pallas_call(
        functools.partial(_ffn_kernel, k0=k0, post=post, k1=k1),
        grid=(t // tm,), in_specs=in_specs, out_specs=out_specs, out_shape=out_shape,
        compiler_params=_params(), name="half_ffn_" + post,
    )(*args)


CONV_K_PAD = 32


def _split_bf16(m):
    hi = jnp.asarray(m, F32).astype(BF16)
    lo = (jnp.asarray(m, F32) - hi.astype(F32)).astype(BF16)
    return hi, lo


def _dft_consts(seg):
    n = 2 * seg
    t = np.arange(seg)
    r = np.arange(seg)
    th = 2.0 * np.pi * np.outer(r, t) / n
    fwd = np.concatenate([np.cos(th), np.sin(th)], axis=0)
    fwd[seg] = np.cos(np.pi * t)
    inv = np.concatenate([2.0 * np.cos(th.T), 2.0 * np.sin(th.T)], axis=1) / n
    inv[:, 0] = 1.0 / n
    inv[:, seg] = np.cos(np.pi * t) / n
    shift = CONV_K // 2 - np.arange(CONV_K)
    ph = 2.0 * np.pi * np.outer(r, shift) / n
    re, im = np.cos(ph), -np.sin(ph)
    re_nyq = re.copy()
    re_nyq[0] = np.cos(np.pi * shift)
    taps = np.zeros((3 * seg, CONV_K_PAD))
    taps[:, :CONV_K] = np.concatenate([re, im, re_nyq], axis=0)
    f2 = jnp.concatenate(_split_bf16(fwd), axis=1)
    g2 = jnp.concatenate(_split_bf16(inv), axis=1)
    return f2, g2, jnp.asarray(taps, F32)


def _conv_kernel(h_ref, wa_ref, wb_ref, ba_ref, bb_ref, dww_ref, dwb_ref, lng_ref, lnb_ref, wo_ref,
                 f2_ref, g2_ref, taps_ref, o_ref, coef_ref, conv_ref, *, seg):
    tm = h_ref.shape[0]

    @pl.when(pl.program_id(0) == 0)
    def _():
        coef_ref[...] = jnp.dot(taps_ref[...], dww_ref[...], precision=lax.Precision.HIGHEST,
                                preferred_element_type=F32)

    hb = h_ref[...]
    a = _dot(hb, wa_ref[...]) + ba_ref[...]
    b = _dot(hb, wb_ref[...]) + bb_ref[...]
    ub = (a * _sigmoid(b)).astype(BF16)
    h_re = coef_ref[0:seg, :]
    h_im = coef_ref[seg:2 * seg, :]
    h_re_nyq = coef_ref[2 * seg:3 * seg, :]
    for s in range(tm // seg):
        us = ub[s * seg:(s + 1) * seg, :]
        spec = _dot(f2_ref[...], jnp.concatenate([us, us], axis=0))
        co, si = spec[:seg, :], spec[seg:, :]
        y = jnp.concatenate([co * h_re + si * h_im, si * h_re_nyq - co * h_im], axis=0).astype(BF16)
        conv_ref[s * seg:(s + 1) * seg, :] = _dot(g2_ref[...], jnp.concatenate([y, y], axis=0)) + dwb_ref[...]

    v = conv_ref[...]
    mu = jnp.mean(v, axis=-1, keepdims=True)
    vc = v - mu
    var = jnp.mean(vc * vc, axis=-1, keepdims=True)
    vn = vc * lax.rsqrt(var + EPS) * lng_ref[...] + lnb_ref[...]
    act = vn * _sigmoid(vn)
    o_ref[...] = _dot(act.astype(BF16), wo_ref[...]).astype(BF16)


def _conv_branch(h, wa, wb, ba, bb, dww, dwb, lng, lnb, wo, *, tm, seg):
    t = h.shape[0]
    assert 2 * seg >= seg + CONV_K - 1
    f2, g2, taps = _dft_consts(seg)
    dww_pad = jnp.zeros((CONV_K_PAD, CONV_DIM), F32).at[:CONV_K].set(dww)
    row = pl.BlockSpec((tm, D_MODEL), lambda i: (i, 0))
    vec = _const_spec((1, CONV_DIM))
    sq = _const_spec((D_MODEL, CONV_DIM))
    return pl.pallas_call(
        functools.partial(_conv_kernel, seg=seg),
        grid=(t // tm,),
        in_specs=[row, sq, sq, vec, vec, _const_spec((CONV_K_PAD, CONV_DIM)), vec, vec, vec, sq,
                  _const_spec(f2.shape), _const_spec(g2.shape), _const_spec(taps.shape)],
        out_specs=row,
        out_shape=jax.ShapeDtypeStruct((t, D_MODEL), BF16),
        scratch_shapes=[pltpu.VMEM((3 * seg, CONV_DIM), F32),
                        pltpu.VMEM((tm, CONV_DIM), F32)],
        compiler_params=_params(), name="conv_branch",
    )(h, wa, wb, ba.reshape(1, -1), bb.reshape(1, -1), dww_pad, dwb.reshape(1, -1), lng.reshape(1, -1),
      lnb.reshape(1, -1), wo, f2, g2, taps)


FEAT_ROWS = 64
FEAT_COLS = 512
FEAT_STRIDE = 9
SUBLANES = 8
FEAT_SPAN = SUBLANES * FEAT_STRIDE


def _aligned(v, m):
    return v if isinstance(v, int) else pl.multiple_of(v, m)


def _feat_kernel(hp_ref, h_ref, hn_ref, wx_ref, bx_ref, wdt_ref, bdt_ref, cw_ref, cb_ref,
                 xbc_ref, dt_ref, hfull_ref, raw0_ref, raw1_ref, act0_ref, act1_ref, *, lseq):
    tm = h_ref.shape[0]
    halo = BF16_ROWS
    i = pl.program_id(0)
    hfull_ref[0:halo, :] = hp_ref[...]
    hfull_ref[halo:halo + tm, :] = h_ref[...]
    hfull_ref[halo + tm:, :] = hn_ref[...]
    t0 = i * tm
    seq_lo = (t0 // lseq) * lseq
    halo_iota = lax.broadcasted_iota(jnp.int32, (halo, 1), 0)
    keep_lo = (t0 - halo + halo_iota) >= seq_lo
    keep_hi = (t0 + tm + halo_iota) < seq_lo + lseq
    first = halo - SSD_CONV_K // 2
    n_lane_blk = FEAT_COLS // LANES
    n_span = tm // FEAT_SPAN

    def project(j, raw_ref):
        cols = pl.ds(_aligned(j * FEAT_COLS, FEAT_COLS), FEAT_COLS)
        raw = _dot(hfull_ref[...], wx_ref[:, cols]) + bx_ref[:, cols]
        for lb in range(n_lane_blk):
            part = raw[:, lb * LANES:(lb + 1) * LANES]
            raw_ref[lb, 0:halo, :] = jnp.where(keep_lo, part[0:halo, :], 0.0)
            raw_ref[lb, halo:halo + tm, :] = part[halo:halo + tm, :]
            raw_ref[lb, halo + tm:, :] = jnp.where(keep_hi, part[halo + tm:, :], 0.0)

    def conv(j, raw_ref, act_ref):
        for lb in range(n_lane_blk):
            dst = pl.ds(_aligned(j * FEAT_COLS + lb * LANES, LANES), LANES)
            taps = [cw_ref[k:k + 1, dst] for k in range(SSD_CONV_K)]
            bias = cb_ref[:, dst]
            for r in range(n_span * FEAT_STRIDE):
                r0 = (r // FEAT_STRIDE) * FEAT_SPAN + r % FEAT_STRIDE
                acc = bias
                for k in range(SSD_CONV_K):
                    acc = acc + raw_ref[lb, pl.ds(r0 + first + k, SUBLANES, stride=FEAT_STRIDE), :] * taps[k]
                act_ref[lb, pl.ds(r0, SUBLANES, stride=FEAT_STRIDE), :] = acc * _sigmoid(acc)
            for r0 in range(n_span * FEAT_SPAN, tm, FEAT_ROWS):
                nr = min(FEAT_ROWS, tm - r0)
                acc = bias
                for k in range(SSD_CONV_K):
                    acc = acc + raw_ref[lb, r0 + first + k:r0 + first + k + nr, :] * taps[k]
                act_ref[lb, r0:r0 + nr, :] = acc * _sigmoid(acc)
            xbc_ref[:, dst] = act_ref[lb].astype(BF16)

    n_blk = XBC_DIM // FEAT_COLS
    project(0, raw0_ref)

    def body(jj, carry):
        j = 2 * jj
        project(j + 1, raw1_ref)
        conv(j, raw0_ref, act0_ref)
        project(j + 2, raw0_ref)
        conv(j + 1, raw1_ref, act1_ref)
        return carry

    lax.fori_loop(0, (n_blk - 2) // 2, body, 0)
    project(n_blk - 1, raw1_ref)
    conv(n_blk - 2, raw0_ref, act0_ref)
    conv(n_blk - 1, raw1_ref, act1_ref)

    dtr = _dot(h_ref[...], wdt_ref[...]) + bdt_ref[...]
    sp = jnp.maximum(dtr, 0.0) + jnp.log1p(jnp.exp(-jnp.abs(dtr)))
    lane = lax.broadcasted_iota(jnp.int32, (1, LANES), 1)
    dt_ref[...] = jnp.where(lane < N_DIR * N_HEADS, sp, 0.0)


def _ssd_features(h, wx, bx, wdt, bdt, cw, cb, *, tm, lseq):
    t = h.shape[0]
    nb = tm // BF16_ROWS
    last = t // BF16_ROWS - 1
    return pl.pallas_call(
        functools.partial(_feat_kernel, lseq=lseq),
        grid=(t // tm,),
        in_specs=[pl.BlockSpec((BF16_ROWS, D_MODEL), lambda i: (jnp.maximum(i * nb - 1, 0), 0)),
                  pl.BlockSpec((tm, D_MODEL), lambda i: (i, 0)),
                  pl.BlockSpec((BF16_ROWS, D_MODEL), lambda i: (jnp.minimum((i + 1) * nb, last), 0)),
                  _const_spec((D_MODEL, XBC_DIM)), _const_spec((1, XBC_DIM)),
                  _const_spec((D_MODEL, LANES)), _const_spec((1, LANES)),
                  _const_spec((SSD_CONV_K, XBC_DIM)), _const_spec((1, XBC_DIM))],
        out_specs=[pl.BlockSpec((tm, XBC_DIM), lambda i: (i, 0)),
                   pl.BlockSpec((tm, LANES), lambda i: (i, 0))],
        out_shape=[jax.ShapeDtypeStruct((t, XBC_DIM), BF16), jax.ShapeDtypeStruct((t, LANES), F32)],
        scratch_shapes=[pltpu.VMEM((tm + 2 * BF16_ROWS, D_MODEL), BF16),
                        pltpu.VMEM((FEAT_COLS // LANES, tm + 2 * BF16_ROWS, LANES), F32),
                        pltpu.VMEM((FEAT_COLS // LANES, tm + 2 * BF16_ROWS, LANES), F32),
                        pltpu.VMEM((FEAT_COLS // LANES, tm, LANES), F32),
                        pltpu.VMEM((FEAT_COLS // LANES, tm, LANES), F32)],
        compiler_params=_params(), name="ssd_features",
    )(h, h, h, wx, bx.reshape(1, -1), wdt, bdt.reshape(1, -1), cw, cb.reshape(1, -1))


def _scan_kernel(xs_ref, bm_ref, cm_ref, dt_ref, a_ref, s0_ref, y_ref, sf_ref, state_ref, *, reverse, d):
    tb = xs_ref.shape[0]
    nch = tb // CHUNK
    j = pl.program_id(1)

    @pl.when(j == 0)
    def _():
        state_ref[...] = s0_ref[0]

    row_i = lax.broadcasted_iota(jnp.int32, (CHUNK, CHUNK), 0)
    col_i = lax.broadcasted_iota(jnp.int32, (CHUNK, CHUNK), 1)
    causal = (col_i >= row_i) if reverse else (row_i >= col_i)
    tri = causal.astype(F32)
    edge = 0 if reverse else CHUNK - 1
    lo64 = lax.broadcasted_iota(jnp.int32, (1, LANES), 1) < HEAD_DIM

    def pair(v0, v1):
        return jnp.where(lo64, v0, v1)

    def chunk_body(c, carry):
        ci = (nch - 1 - c) if reverse else c
        r0 = pl.multiple_of(ci * CHUNK, CHUNK)
        rows = pl.ds(r0, CHUNK)
        dt = dt_ref[rows, :]
        acum = jnp.dot(tri, dt * (a_ref[...] * LOG2E), precision=lax.Precision.HIGHEST,
                       preferred_element_type=F32)
        acum_t = acum.T
        dt_t = dt.T
        a_edge_t = jnp.broadcast_to(acum_t[:, edge:edge + 1], (LANES, CHUNK))
        w_t = (jnp.exp2(a_edge_t - acum_t) * dt_t).astype(BF16)
        dt_tb = dt_t.astype(BF16)
        pre = []
        for g in range(N_GROUPS):
            bg = bm_ref[rows, g * D_STATE:(g + 1) * D_STATE]
            cg = cm_ref[rows, g * D_STATE:(g + 1) * D_STATE]
            bg_t = bg.astype(F32).T.astype(BF16)
            cb = jnp.where(causal, _dot(cg, bg_t), 0.0).astype(BF16)
            s_in = state_ref[g]
            pre.append((bg_t, cb, s_in, _dot(cg, s_in.astype(BF16))))
        zeros_half = jnp.zeros((CHUNK, LANES), BF16)
        for g in range(N_GROUPS):
            bg_t, cb, s_in, y_off = pre[g]
            m_parts, bw_parts, a_cols, rhs_parts = [], [], [], []
            for hh in range(HEADS_PER_GROUP):
                q = d * N_HEADS + g * HEADS_PER_GROUP + hh
                a_col = jnp.broadcast_to(acum[:, q:q + 1], (CHUNK, CHUNK))
                decay = jnp.exp2(jnp.minimum(a_col - acum_t[q:q + 1, :], 0.0))
                m_parts.append(cb * decay.astype(BF16) * dt_tb[q:q + 1, :])
                bw_parts.append(bg_t * w_t[q:q + 1, :])
                a_cols.append(a_col)
                half = xs_ref[rows, g * GROUP_W + (hh // 2) * LANES:g * GROUP_W + (hh // 2 + 1) * LANES]
                own = jnp.where(lo64 if hh % 2 == 0 else jnp.logical_not(lo64), half, zeros_half)
                rhs_parts.append(jnp.concatenate([own, zeros_half] if hh < 2 else [zeros_half, own], axis=1))
            lhs = jnp.concatenate([jnp.concatenate(m_parts, axis=1), jnp.concatenate(bw_parts, axis=1)], axis=0)
            res = _dot(lhs, jnp.concatenate(rhs_parts, axis=0))
            e_grp = jnp.exp2(jnp.concatenate([pair(a_cols[0], a_cols[1]), pair(a_cols[2], a_cols[3])], axis=1))
            y = res[:CHUNK, :] + y_off * e_grp
            y_ref[rows, g * GROUP_W:(g + 1) * GROUP_W] = y.astype(BF16)
            state_ref[g] = s_in * e_grp[edge:edge + 1, :] + res[CHUNK:, :]
        return carry

    lax.fori_loop(0, nch, chunk_body, 0)

    @pl.when(j == pl.num_programs(1) - 1)
    def _():
        sf_ref[0] = state_ref[...]


def _ssd_scan(xbc, dt, a128, s0, *, bsz, lseq, tb, reverse, d):
    nblk = lseq // tb

    def blk(b, j):
        return b * nblk + ((nblk - 1 - j) if reverse else j)

    n_xs = D_INNER // (N_GROUPS * D_STATE)
    st_spec = pl.BlockSpec((1, N_GROUPS, D_STATE, GROUP_W), lambda b, j: (b, 0, 0, 0))
    return pl.pallas_call(
        functools.partial(_scan_kernel, reverse=reverse, d=d),
        grid=(bsz, nblk),
        in_specs=[pl.BlockSpec((tb, D_INNER), lambda b, j: (blk(b, j), 0)),
                  pl.BlockSpec((tb, N_GROUPS * D_STATE), lambda b, j: (blk(b, j), n_xs)),
                  pl.BlockSpec((tb, N_GROUPS * D_STATE), lambda b, j: (blk(b, j), n_xs + 1)),
                  pl.BlockSpec((tb, LANES), lambda b, j: (blk(b, j), 0)),
                  pl.BlockSpec((1, LANES), lambda b, j: (0, 0)),
                  st_spec],
        out_specs=[pl.BlockSpec((tb, D_INNER), lambda b, j: (blk(b, j), 0)), st_spec],
        out_shape=[jax.ShapeDtypeStruct((bsz * lseq, D_INNER), BF16),
                   jax.ShapeDtypeStruct((bsz, N_GROUPS, D_STATE, GROUP_W), F32)],
        scratch_shapes=[pltpu.VMEM((N_GROUPS, D_STATE, GROUP_W), F32)],
        compiler_params=_params(2), name="ssd_scan_bwd" if reverse else "ssd_scan_fwd",
    )(xbc, xbc, xbc, dt, a128, s0)


def _merge_kernel(x_ref, h_ref, yf_ref, yb_ref, xs_ref, yc_ref, mod_ref, wz_ref, bz_ref, wg_ref, bg_ref,
                  dsk_ref, nrm_ref, wso_ref, wo_ref, o_ref, *, k_gate):
    hb = h_ref[...]
    z = _dot(hb, wz_ref[...]) + bz_ref[...]
    y = yf_ref[...].astype(F32) + yb_ref[...].astype(F32) + dsk_ref[...] * xs_ref[...].astype(F32)
    y = y * (z * _sigmoid(z))
    parts = []
    for g in range(N_GROUPS):
        yg = y[:, g * GROUP_W:(g + 1) * GROUP_W]
        ms = jnp.mean(yg * yg, axis=-1, keepdims=True)
        parts.append(yg * lax.rsqrt(ms + EPS))
    yn = jnp.concatenate(parts, axis=1) * nrm_ref[...]
    y_s = _dot(yn.astype(BF16), wso_ref[...])
    graw = _dot(hb, wg_ref[...]) + bg_ref[...]
    m = _sigmoid(graw[:, :D_MODEL]) * yc_ref[...].astype(F32) + _sigmoid(graw[:, D_MODEL:]) * y_s
    mix = _dot(m.astype(BF16), wo_ref[...])
    o_ref[...] = x_ref[...] + mod_ref[0, k_gate:k_gate + 1, :] * mix


def _merge(x, h, yf, yb, xbc, yc, mods, mod_map, wz, bz, wg, bg, dsk, nrm, wso, wo, *, tm, k_gate):
    t = x.shape[0]
    row = pl.BlockSpec((tm, D_MODEL), lambda i: (i, 0))
    wide = pl.BlockSpec((tm, D_INNER), lambda i: (i, 0))
    return pl.pallas_call(
        functools.partial(_merge_kernel, k_gate=k_gate),
        grid=(t // tm,),
        in_specs=[row, row, wide, wide, wide, row,
                  pl.BlockSpec((1, N_MOD, D_MODEL), mod_map),
                  _const_spec((D_MODEL, D_INNER)), _const_spec((1, D_INNER)),
                  _const_spec((D_MODEL, 2 * D_MODEL)), _const_spec((1, 2 * D_MODEL)),
                  _const_spec((1, D_INNER)), _const_spec((1, D_INNER)),
                  _const_spec((D_INNER, D_MODEL)), _const_spec((D_MODEL, D_MODEL))],
        out_specs=row,
        out_shape=jax.ShapeDtypeStruct((t, D_MODEL), F32),
        compiler_params=_params(), name="mixer_merge",
    )(x, h, yf, yb, xbc, yc, mods, wz, bz.reshape(1, -1), wg, bg.reshape(1, -1), dsk.reshape(1, -1),
      nrm.reshape(1, -1), wso, wo)


def kernel(x, c, ctx, c_ctx, w_ada, b_ada, ffn1_norm, ffn1_w_gate, ffn1_w_up, ffn1_w_down, mix_norm, w_in, b_in, conv_dw_w, conv_dw_b, conv_ln_g, conv_ln_b, conv_w_out, ssd_conv_w, ssd_conv_b, ssd_dt_bias, ssd_a_log, ssd_d, ssd_norm, ssd_w_out, w_out, ffn2_norm, ffn2_w_gate, ffn2_w_up, ffn2_w_down, final_norm):
    bsz, seq, _ = x.shape
    ctx_len = ctx.shape[1]
    depth = w_ada.shape[0]
    assert seq % 512 == 0 and ctx_len % 256 == 0 and bsz + 1 <= 8

    tm_x, tm_c = 512, 256
    xf = x.reshape(bsz * seq, D_MODEL)
    cf = ctx.reshape(bsz * ctx_len, D_MODEL)
    x_map = lambda i: ((i * tm_x) // seq, 0, 0)
    x_map_half = lambda i: ((i * (tm_x // 2)) // seq, 0, 0)
    c_map = lambda i: (bsz, 0, 0)

    cvec = jnp.zeros((8, D_MODEL), F32).at[:bsz].set(c).at[bsz].set(c_ctx)
    mods_all = _mods(cvec, w_ada, b_ada)[:, :bsz + 1].reshape(depth, bsz + 1, N_MOD, D_MODEL)

    s1 = 2 * CONV_DIM
    s2 = s1 + D_INNER
    s3 = s2 + XBC_DIM
    s4 = s3 + N_DIR * N_HEADS
    zero_state = jnp.zeros((bsz, N_GROUPS, D_STATE, GROUP_W), F32)

    for i in range(depth):
        last = i == depth - 1
        mods = mods_all[i]
        bf = lambda w: w.astype(BF16)
        w1 = (bf(ffn1_w_gate[i]), bf(ffn1_w_up[i]), bf(ffn1_w_down[i]))
        w2 = (bf(ffn2_w_gate[i]), bf(ffn2_w_up[i]), bf(ffn2_w_down[i]))
        wi, bi = w_in[i], b_in[i]
        wa, wb = bf(wi[:, :CONV_DIM]), bf(wi[:, CONV_DIM:s1])
        ba, bb = bi[:CONV_DIM], bi[CONV_DIM:s1]
        wz, bz = bf(wi[:, s1:s2]), bi[s1:s2]
        wx, bx = bf(wi[:, s2:s3]), bi[s2:s3]
        n_dt = N_DIR * N_HEADS
        wdt = bf(jnp.zeros((D_MODEL, LANES), F32).at[:, :n_dt].set(wi[:, s3:s4]))
        pad_dt = lambda v: jnp.zeros((LANES,), F32).at[:n_dt].set(v.reshape(-1))
        bdt = pad_dt(bi[s3:s4] + ssd_dt_bias[i].reshape(-1))
        a128 = pad_dt(-jnp.exp(ssd_a_log[i].astype(F32))).reshape(1, LANES)
        wgt, bgt = bf(wi[:, s4:]), bi[s4:]
        dsk = jnp.repeat(ssd_d[i], HEAD_DIM)
        conv_w = (wa, wb, ba, bb, conv_dw_w[i], conv_dw_b[i], conv_ln_g[i], conv_ln_b[i], bf(conv_w_out[i]))
        merge_w = (wz, bz, wgt, bgt, dsk, ssd_norm[i], bf(ssd_w_out[i]), bf(w_out[i]))
        feat_w = (wx, bx, wdt, bdt, ssd_conv_w[i], ssd_conv_b[i])

        xf, hx = _ffn(xf, mods, x_map, ffn1_norm[i], *w1, tm=tm_x, k0=0, post="mod", g2=mix_norm[i], k1=3)
        cf, hc = _ffn(cf, mods, c_map, ffn1_norm[i], *w1, tm=tm_c, k0=0, post="mod", g2=mix_norm[i], k1=3)

        xbc_c, dt_c = _ssd_features(hc, *feat_w, tm=tm_c, lseq=ctx_len)
        xbc_x, dt_x = _ssd_features(hx, *feat_w, tm=tm_x, lseq=seq)
        scan_c = functools.partial(_ssd_scan, xbc_c, dt_c, a128, bsz=bsz, lseq=ctx_len, tb=256)
        scan_x = functools.partial(_ssd_scan, xbc_x, dt_x, a128, bsz=bsz, lseq=seq, tb=512)
        yf_c, s_f = scan_c(zero_state, reverse=False, d=0)
        yb_c, s_b = scan_c(zero_state, reverse=True, d=1)
        yf_x, _ = scan_x(s_f, reverse=False, d=0)
        yb_x, _ = scan_x(s_b, reverse=True, d=1)

        yc_x = _conv_branch(hx, *conv_w, tm=tm_x // 2, seg=GRID_W)
        xf = _merge(xf, hx, yf_x, yb_x, xbc_x, yc_x, mods, x_map_half, *merge_w, tm=tm_x // 2, k_gate=5)
        if not last:
            yc_c = _conv_branch(hc, *conv_w, tm=tm_c, seg=ctx_len)
            cf = _merge(cf, hc, yf_c, yb_c, xbc_c, yc_c, mods, c_map, *merge_w, tm=tm_c, k_gate=5)

        if last:
            xf = _ffn(xf, mods, x_map, ffn2_norm[i], *w2, tm=tm_x, k0=6, post="final", g2=final_norm)
        else:
            xf = _ffn(xf, mods, x_map, ffn2_norm[i], *w2, tm=tm_x, k0=6)
            cf = _ffn(cf, mods, c_map, ffn2_norm[i], *w2, tm=tm_c, k0=6)
    return xf.reshape(bsz, seq, D_MODEL)
```

```python
import functools

import jax
import jax.numpy as jnp
import numpy as np
from jax import lax
from jax.experimental import pallas as pl
from jax.experimental.pallas import tpu as pltpu

F32 = jnp.float32
BF16 = jnp.bfloat16

D_MODEL = 1024
N_MOD = 9
D_FF = 2816
CONV_DIM = D_MODEL
CONV_K = 31
D_INNER = 2 * D_MODEL
HEAD_DIM = 64
N_HEADS = D_INNER // HEAD_DIM
N_GROUPS = 8
HEADS_PER_GROUP = N_HEADS // N_GROUPS
GROUP_W = HEADS_PER_GROUP * HEAD_DIM
D_STATE = 128
SSD_CONV_K = 5
CHUNK = 128
N_DIR = 2
XBC_DIM = D_INNER + 2 * N_GROUPS * D_STATE
GRID_W = 64
EPS = 1e-6

LANES = 128
BF16_ROWS = 16
FF_CHUNK = 256
VMEM_LIMIT = 58 * 1024 * 1024


def _dot(a, b):
    return jnp.dot(a, b, preferred_element_type=F32)


LOG2E = 1.4426950408889634


def _sigmoid(v):
    return 1.0 / (1.0 + jnp.exp2(v * (-LOG2E)))


def _const_spec(shape):
    nd = len(shape)
    return pl.BlockSpec(shape, lambda *_: (0,) * nd, pipeline_mode=pl.Buffered(1))


def _params(n_axes=1, flags=None):
    return pltpu.CompilerParams(dimension_semantics=("arbitrary",) * n_axes,
                                vmem_limit_bytes=VMEM_LIMIT, flags=flags)


def _mods_kernel(c_ref, w_ref, b_ref, o_ref):
    c = c_ref[...]
    act = c * _sigmoid(c)
    o_ref[0] = jnp.dot(act, w_ref[0], precision=lax.Precision.HIGHEST,
                       preferred_element_type=F32) + b_ref[0]


def _mods(cvec, w_ada, b_ada):
    depth, _, n_out = w_ada.shape
    tn = D_MODEL
    return pl.pallas_call(
        _mods_kernel,
        grid=(depth, n_out // tn),
        in_specs=[pl.BlockSpec((8, D_MODEL), lambda l, j: (0, 0)),
                  pl.BlockSpec((1, D_MODEL, tn), lambda l, j: (l, 0, j)),
                  pl.BlockSpec((1, 1, tn), lambda l, j: (l, 0, j))],
        out_specs=pl.BlockSpec((1, 8, tn), lambda l, j: (l, 0, j)),
        out_shape=jax.ShapeDtypeStruct((depth, 8, n_out), F32),
        compiler_params=_params(2),
        name="ada_mods",
    )(cvec, w_ada, b_ada.reshape(depth, 1, n_out))


def _rms(v, g):
    ms = jnp.mean(v * v, axis=-1, keepdims=True)
    return v * lax.rsqrt(ms + EPS) * g


def _ffn_kernel(*refs, k0, post, k1):
    if post == "mod":
        x_ref, mod_ref, g_ref, wg_ref, wu_ref, wd_ref, g2_ref, o_ref, h_ref = refs
    elif post == "final":
        x_ref, mod_ref, g_ref, wg_ref, wu_ref, wd_ref, g2_ref, o_ref = refs
    else:
        x_ref, mod_ref, g_ref, wg_ref, wu_ref, wd_ref, o_ref = refs
    x = x_ref[...]
    shift = mod_ref[0, k0:k0 + 1, :]
    scale = mod_ref[0, k0 + 1:k0 + 2, :]
    gate = mod_ref[0, k0 + 2:k0 + 3, :]
    hb = (_rms(x, g_ref[...]) * (1.0 + scale) + shift).astype(BF16)
    n_chunks = D_FF // FF_CHUNK

    def gate_up(c):
        sl = slice(c * FF_CHUNK, (c + 1) * FF_CHUNK)
        return _dot(hb, wg_ref[:, sl]), _dot(hb, wu_ref[:, sl])

    acc = jnp.zeros(x.shape, F32)
    pending = gate_up(0)
    for c in range(n_chunks):
        g, u = pending
        if c + 1 < n_chunks:
            pending = gate_up(c + 1)
        a = (g * _sigmoid(g)) * u
        acc = acc + _dot(a.astype(BF16), wd_ref[c * FF_CHUNK:(c + 1) * FF_CHUNK, :])
    y = x + 0.5 * gate * acc
    if post == "final":
        o_ref[...] = _rms(y, g2_ref[...])
        return
    o_ref[...] = y
    if post == "mod":
        shift2 = mod_ref[0, k1:k1 + 1, :]
        scale2 = mod_ref[0, k1 + 1:k1 + 2, :]
        h_ref[...] = (_rms(y, g2_ref[...]) * (1.0 + scale2) + shift2).astype(BF16)


def _ffn(x, mods, mod_map, norm_g, wg, wu, wd, *, tm, k0, post="none", g2=None, k1=0):
    t = x.shape[0]
    row = pl.BlockSpec((tm, D_MODEL), lambda i: (i, 0))
    in_specs = [row,
                pl.BlockSpec((1, N_MOD, D_MODEL), mod_map),
                _const_spec((1, D_MODEL)),
                _const_spec((D_MODEL, D_FF)), _const_spec((D_MODEL, D_FF)),
                _const_spec((D_FF, D_MODEL))]
    args = [x, mods, norm_g.reshape(1, D_MODEL), wg, wu, wd]
    if post != "none":
        in_specs.append(_const_spec((1, D_MODEL)))
        args.append(g2.reshape(1, D_MODEL))
    if post == "mod":
        out_specs = [row, row]
        out_shape = [jax.ShapeDtypeStruct((t, D_MODEL), F32), jax.ShapeDtypeStruct((t, D_MODEL), BF16)]
    else:
        out_specs = row
        out_shape = jax.ShapeDtypeStruct((t, D_MODEL), F32)
    return pl.pallas_call(
        functools.partial(_ffn_kernel, k0=k0, post=post, k1=k1),
        grid=(t // tm,), in_specs=in_specs, out_specs=out_specs, out_shape=out_shape,
        compiler_params=_params(), name="half_ffn_" + post,
    )(*args)


CONV_K_PAD = 32
CONV_SUB = 256


def _split_bf16(m):
    hi = jnp.asarray(m, F32).astype(BF16)
    lo = (jnp.asarray(m, F32) - hi.astype(F32)).astype(BF16)
    return hi, lo


def _dft_consts(seg):
    n = 2 * seg
    t = np.arange(seg)
    r = np.arange(seg)
    th = 2.0 * np.pi * np.outer(r, t) / n
    fwd = np.concatenate([np.cos(th), np.sin(th)], axis=0)
    fwd[seg] = np.cos(np.pi * t)
    inv = np.concatenate([2.0 * np.cos(th.T), 2.0 * np.sin(th.T)], axis=1) / n
    inv[:, 0] = 1.0 / n
    inv[:, seg] = np.cos(np.pi * t) / n
    shift = CONV_K // 2 - np.arange(CONV_K)
    ph = 2.0 * np.pi * np.outer(r, shift) / n
    re, im = np.cos(ph), -np.sin(ph)
    re_nyq = re.copy()
    re_nyq[0] = np.cos(np.pi * shift)
    taps = np.zeros((3 * seg, CONV_K_PAD))
    taps[:, :CONV_K] = np.concatenate([re, im, re_nyq], axis=0)
    f2 = jnp.concatenate(_split_bf16(fwd), axis=1)
    g2 = jnp.concatenate(_split_bf16(inv), axis=1)
    return f2, g2, jnp.asarray(taps, F32)


def _conv_kernel(h_ref, wa_ref, wb_ref, ba_ref, bb_ref, dww_ref, dwb_ref, lng_ref, lnb_ref, wo_ref,
                 f2_ref, g2_ref, taps_ref, o_ref, coef_ref, conv_ref, *, seg):
    tm = h_ref.shape[0]

    @pl.when(pl.program_id(0) == 0)
    def _():
        coef_ref[...] = jnp.dot(taps_ref[...], dww_ref[...], precision=lax.Precision.HIGHEST,
                                preferred_element_type=F32)

    sub = max(seg, min(tm, CONV_SUB))
    subs = [slice(r, r + sub) for r in range(0, tm, sub)]
    glu = []
    for rs in subs:
        hb = h_ref[rs, :]
        glu.append((_dot(hb, wa_ref[...]) + ba_ref[...], _dot(hb, wb_ref[...]) + bb_ref[...]))
    specs = []
    for a, b in glu:
        ub = (a * _sigmoid(b)).astype(BF16)
        for s in range(sub // seg):
            us = ub[s * seg:(s + 1) * seg, :]
            specs.append(_dot(f2_ref[...], jnp.concatenate([us, us], axis=0)))
    h_re = coef_ref[0:seg, :]
    h_im = coef_ref[seg:2 * seg, :]
    h_re_nyq = coef_ref[2 * seg:3 * seg, :]
    for s, spec in enumerate(specs):
        co, si = spec[:seg, :], spec[seg:, :]
        y = jnp.concatenate([co * h_re + si * h_im, si * h_re_nyq - co * h_im], axis=0).astype(BF16)
        conv_ref[s * seg:(s + 1) * seg, :] = _dot(g2_ref[...], jnp.concatenate([y, y], axis=0)) + dwb_ref[...]

    for rs in subs:
        v = conv_ref[rs, :]
        mu = jnp.mean(v, axis=-1, keepdims=True)
        vc = v - mu
        var = jnp.mean(vc * vc, axis=-1, keepdims=True)
        vn = vc * lax.rsqrt(var + EPS) * lng_ref[...] + lnb_ref[...]
        act = vn * _sigmoid(vn)
        o_ref[rs, :] = _dot(act.astype(BF16), wo_ref[...]).astype(BF16)


def _conv_branch(h, wa, wb, ba, bb, dww, dwb, lng, lnb, wo, *, tm, seg):
    t = h.shape[0]
    assert 2 * seg >= seg + CONV_K - 1
    f2, g2, taps = _dft_consts(seg)
    dww_pad = jnp.zeros((CONV_K_PAD, CONV_DIM), F32).at[:CONV_K].set(dww)
    row = pl.BlockSpec((tm, D_MODEL), lambda i: (i, 0))
    vec = _const_spec((1, CONV_DIM))
    sq = _const_spec((D_MODEL, CONV_DIM))
    return pl.pallas_call(
        functools.partial(_conv_kernel, seg=seg),
        grid=(t // tm,),
        in_specs=[row, sq, sq, vec, vec, _const_spec((CONV_K_PAD, CONV_DIM)), vec, vec, vec, sq,
                  _const_spec(f2.shape), _const_spec(g2.shape), _const_spec(taps.shape)],
        out_specs=row,
        out_shape=jax.ShapeDtypeStruct((t, D_MODEL), BF16),
        scratch_shapes=[pltpu.VMEM((3 * seg, CONV_DIM), F32),
                        pltpu.VMEM((tm, CONV_DIM), F32)],
        compiler_params=_params(), name="conv_branch",
    )(h, wa, wb, ba.reshape(1, -1), bb.reshape(1, -1), dww_pad, dwb.reshape(1, -1), lng.reshape(1, -1),
      lnb.reshape(1, -1), wo, f2, g2, taps)


FEAT_ROWS = 64
FEAT_COLS = 512
FEAT_STRIDE = 9
SUBLANES = 8
FEAT_SPAN = SUBLANES * FEAT_STRIDE


def _aligned(v, m):
    return v if isinstance(v, int) else pl.multiple_of(v, m)


def _feat_kernel(hp_ref, h_ref, hn_ref, wx_ref, bx_ref, wdt_ref, bdt_ref, cw_ref, cb_ref,
                 xbc_ref, dt_ref, hfull_ref, raw0_ref, raw1_ref, act0_ref, act1_ref, *, lseq):
    tm = h_ref.shape[0]
    halo = BF16_ROWS
    i = pl.program_id(0)
    hfull_ref[0:halo, :] = hp_ref[...]
    hfull_ref[halo:halo + tm, :] = h_ref[...]
    hfull_ref[halo + tm:, :] = hn_ref[...]
    t0 = i * tm
    seq_lo = (t0 // lseq) * lseq
    halo_iota = lax.broadcasted_iota(jnp.int32, (halo, 1), 0)
    keep_lo = (t0 - halo + halo_iota) >= seq_lo
    keep_hi = (t0 + tm + halo_iota) < seq_lo + lseq
    first = halo - SSD_CONV_K // 2
    n_lane_blk = FEAT_COLS // LANES
    n_span = tm // FEAT_SPAN

    def project(j, raw_ref):
        cols = pl.ds(_aligned(j * FEAT_COLS, FEAT_COLS), FEAT_COLS)
        raw = _dot(hfull_ref[...], wx_ref[:, cols]) + bx_ref[:, cols]
        for lb in range(n_lane_blk):
            part = raw[:, lb * LANES:(lb + 1) * LANES]
            raw_ref[lb, 0:halo, :] = jnp.where(keep_lo, part[0:halo, :], 0.0)
            raw_ref[lb, halo:halo + tm, :] = part[halo:halo + tm, :]
            raw_ref[lb, halo + tm:, :] = jnp.where(keep_hi, part[halo + tm:, :], 0.0)

    def conv(j, raw_ref, act_ref):
        for lb in range(n_lane_blk):
            dst = pl.ds(_aligned(j * FEAT_COLS + lb * LANES, LANES), LANES)
            taps = [cw_ref[k:k + 1, dst] for k in range(SSD_CONV_K)]
            bias = cb_ref[:, dst]
            for r in range(n_span * FEAT_STRIDE):
                r0 = (r // FEAT_STRIDE) * FEAT_SPAN + r % FEAT_STRIDE
                acc = bias
                for k in range(SSD_CONV_K):
                    acc = acc + raw_ref[lb, pl.ds(r0 + first + k, SUBLANES, stride=FEAT_STRIDE), :] * taps[k]
                act_ref[lb, pl.ds(r0, SUBLANES, stride=FEAT_STRIDE), :] = acc * _sigmoid(acc)
            for r0 in range(n_span * FEAT_SPAN, tm, FEAT_ROWS):
                nr = min(FEAT_ROWS, tm - r0)
                acc = bias
                for k in range(SSD_CONV_K):
                    acc = acc + raw_ref[lb, r0 + first + k:r0 + first + k + nr, :] * taps[k]
                act_ref[lb, r0:r0 + nr, :] = acc * _sigmoid(acc)
            xbc_ref[:, dst] = act_ref[lb].astype(BF16)

    n_blk = XBC_DIM // FEAT_COLS
    project(0, raw0_ref)

    def body(jj, carry):
        j = 2 * jj
        project(j + 1, raw1_ref)
        conv(j, raw0_ref, act0_ref)
        project(j + 2, raw0_ref)
        conv(j + 1, raw1_ref, act1_ref)
        return carry

    lax.fori_loop(0, (n_blk - 2) // 2, body, 0)
    project(n_blk - 1, raw1_ref)
    conv(n_blk - 2, raw0_ref, act0_ref)
    conv(n_blk - 1, raw1_ref, act1_ref)

    dtr = _dot(h_ref[...], wdt_ref[...]) + bdt_ref[...]
    sp = jnp.maximum(dtr, 0.0) + jnp.log1p(jnp.exp(-jnp.abs(dtr)))
    lane = lax.broadcasted_iota(jnp.int32, (1, LANES), 1)
    dt_ref[...] = jnp.where(lane < N_DIR * N_HEADS, sp, 0.0)


def _ssd_features(h, wx, bx, wdt, bdt, cw, cb, *, tm, lseq):
    t = h.shape[0]
    nb = tm // BF16_ROWS
    last = t // BF16_ROWS - 1
    return pl.pallas_call(
        functools.partial(_feat_kernel, lseq=lseq),
        grid=(t // tm,),
        in_specs=[pl.BlockSpec((BF16_ROWS, D_MODEL), lambda i: (jnp.maximum(i * nb - 1, 0), 0)),
                  pl.BlockSpec((tm, D_MODEL), lambda i: (i, 0)),
                  pl.BlockSpec((BF16_ROWS, D_MODEL), lambda i: (jnp.minimum((i + 1) * nb, last), 0)),
                  _const_spec((D_MODEL, XBC_DIM)), _const_spec((1, XBC_DIM)),
                  _const_spec((D_MODEL, LANES)), _const_spec((1, LANES)),
                  _const_spec((SSD_CONV_K, XBC_DIM)), _const_spec((1, XBC_DIM))],
        out_specs=[pl.BlockSpec((tm, XBC_DIM), lambda i: (i, 0)),
                   pl.BlockSpec((tm, LANES), lambda i: (i, 0))],
        out_shape=[jax.ShapeDtypeStruct((t, XBC_DIM), BF16), jax.ShapeDtypeStruct((t, LANES), F32)],
        scratch_shapes=[pltpu.VMEM((tm + 2 * BF16_ROWS, D_MODEL), BF16),
                        pltpu.VMEM((FEAT_COLS // LANES, tm + 2 * BF16_ROWS, LANES), F32),
                        pltpu.VMEM((FEAT_COLS // LANES, tm + 2 * BF16_ROWS, LANES), F32),
                        pltpu.VMEM((FEAT_COLS // LANES, tm, LANES), F32),
                        pltpu.VMEM((FEAT_COLS // LANES, tm, LANES), F32)],
        compiler_params=_params(), name="ssd_features",
    )(h, h, h, wx, bx.reshape(1, -1), wdt, bdt.reshape(1, -1), cw, cb.reshape(1, -1))


def _scan_kernel(xs_ref, bm_ref, cm_ref, dt_ref, a_ref, s0_ref, y_ref, sf_ref, state_ref, *, reverse, d):
    tb = xs_ref.shape[0]
    nch = tb // CHUNK
    j = pl.program_id(1)

    @pl.when(j == 0)
    def _():
        state_ref[...] = s0_ref[0]

    row_i = lax.broadcasted_iota(jnp.int32, (CHUNK, CHUNK), 0)
    col_i = lax.broadcasted_iota(jnp.int32, (CHUNK, CHUNK), 1)
    causal = (col_i >= row_i) if reverse else (row_i >= col_i)
    tri = causal.astype(F32)
    edge = 0 if reverse else CHUNK - 1
    lo64 = lax.broadcasted_iota(jnp.int32, (1, LANES), 1) < HEAD_DIM

    def pair(v0, v1):
        return jnp.where(lo64, v0, v1)

    def chunk_rows(c):
        ci = (nch - 1 - c) if reverse else c
        return pl.ds(pl.multiple_of(ci * CHUNK, CHUNK), CHUNK)

    def decay_terms(c):
        dt = dt_ref[chunk_rows(c), :]
        acum = jnp.dot(tri, dt * (a_ref[...] * LOG2E), precision=lax.Precision.HIGHEST,
                       preferred_element_type=F32)
        acum_t = acum.T
        dt_t = dt.T
        a_edge_t = jnp.broadcast_to(acum_t[:, edge:edge + 1], (LANES, CHUNK))
        w_t = (jnp.exp2(a_edge_t - acum_t) * dt_t).astype(BF16)
        return acum, acum_t, w_t, dt_t.astype(BF16)

    def chunk_body(c, terms):
        rows = chunk_rows(c)
        acum, acum_t, w_t, dt_tb = terms
        next_terms = decay_terms(jnp.minimum(c + 1, nch - 1))
        zeros_half = jnp.zeros((CHUNK, LANES), BF16)
        staged = []
        for g in range(N_GROUPS):
            bg = bm_ref[rows, g * D_STATE:(g + 1) * D_STATE]
            cg = cm_ref[rows, g * D_STATE:(g + 1) * D_STATE]
            bg_t = bg.astype(F32).T.astype(BF16)
            cb = jnp.where(causal, _dot(cg, bg_t), 0.0).astype(BF16)
            s_in = state_ref[g]
            y_off = _dot(cg, s_in.astype(BF16))
            m_parts, bw_parts, a_cols, rhs_parts = [], [], [], []
            for hh in range(HEADS_PER_GROUP):
                q = d * N_HEADS + g * HEADS_PER_GROUP + hh
                a_col = jnp.broadcast_to(acum[:, q:q + 1], (CHUNK, CHUNK))
                decay = jnp.exp2(jnp.minimum(a_col - acum_t[q:q + 1, :], 0.0))
                m_parts.append(cb * decay.astype(BF16) * dt_tb[q:q + 1, :])
                bw_parts.append(bg_t * w_t[q:q + 1, :])
                a_cols.append(a_col)
                half = xs_ref[rows, g * GROUP_W + (hh // 2) * LANES:g * GROUP_W + (hh // 2 + 1) * LANES]
                own = jnp.where(lo64 if hh % 2 == 0 else jnp.logical_not(lo64), half, zeros_half)
                rhs_parts.append(jnp.concatenate([own, zeros_half] if hh < 2 else [zeros_half, own], axis=1))
            lhs = jnp.concatenate([jnp.concatenate(m_parts, axis=1), jnp.concatenate(bw_parts, axis=1)], axis=0)
            e_grp = jnp.exp2(jnp.concatenate([pair(a_cols[0], a_cols[1]), pair(a_cols[2], a_cols[3])], axis=1))
            staged.append((lhs, jnp.concatenate(rhs_parts, axis=0), s_in, y_off, e_grp))
        for g, (lhs, rhs, s_in, y_off, e_grp) in enumerate(staged):
            res = _dot(lhs, rhs)
            y = res[:CHUNK, :] + y_off * e_grp
            y_ref[rows, g * GROUP_W:(g + 1) * GROUP_W] = y.astype(BF16)
            state_ref[g] = s_in * e_grp[edge:edge + 1, :] + res[CHUNK:, :]
        return next_terms

    lax.fori_loop(0, nch, chunk_body, decay_terms(0))

    @pl.when(j == pl.num_programs(1) - 1)
    def _():
        sf_ref[0] = state_ref[...]


def _ssd_scan(xbc, dt, a128, s0, *, bsz, lseq, tb, reverse, d):
    nblk = lseq // tb

    def blk(b, j):
        return b * nblk + ((nblk - 1 - j) if reverse else j)

    n_xs = D_INNER // (N_GROUPS * D_STATE)
    st_spec = pl.BlockSpec((1, N_GROUPS, D_STATE, GROUP_W), lambda b, j: (b, 0, 0, 0))
    return pl.pallas_call(
        functools.partial(_scan_kernel, reverse=reverse, d=d),
        grid=(bsz, nblk),
        in_specs=[pl.BlockSpec((tb, D_INNER), lambda b, j: (blk(b, j), 0)),
                  pl.BlockSpec((tb, N_GROUPS * D_STATE), lambda b, j: (blk(b, j), n_xs)),
                  pl.BlockSpec((tb, N_GROUPS * D_STATE), lambda b, j: (blk(b, j), n_xs + 1)),
                  pl.BlockSpec((tb, LANES), lambda b, j: (blk(b, j), 0)),
                  pl.BlockSpec((1, LANES), lambda b, j: (0, 0)),
                  st_spec],
        out_specs=[pl.BlockSpec((tb, D_INNER), lambda b, j: (blk(b, j), 0)), st_spec],
        out_shape=[jax.ShapeDtypeStruct((bsz * lseq, D_INNER), BF16),
                   jax.ShapeDtypeStruct((bsz, N_GROUPS, D_STATE, GROUP_W), F32)],
        scratch_shapes=[pltpu.VMEM((N_GROUPS, D_STATE, GROUP_W), F32)],
        compiler_params=_params(2), name="ssd_scan_bwd" if reverse else "ssd_scan_fwd",
    )(xbc, xbc, xbc, dt, a128, s0)


MERGE_SUB = 256


def _merge_kernel(x_ref, h_ref, yf_ref, yb_ref, xs_ref, yc_ref, mod_ref, wz_ref, bz_ref, wg_ref, bg_ref,
                  dsk_ref, nrm_ref, wso_ref, wo_ref, o_ref, *, k_gate):
    tm = x_ref.shape[0]
    subs = [slice(r, r + MERGE_SUB) for r in range(0, tm, MERGE_SUB)]
    proj = []
    for rs in subs:
        hb = h_ref[rs, :]
        proj.append((_dot(hb, wz_ref[...]) + bz_ref[...], _dot(hb, wg_ref[...]) + bg_ref[...]))
    y_ssd = []
    for rs, (z, _) in zip(subs, proj):
        y = yf_ref[rs, :].astype(F32) + yb_ref[rs, :].astype(F32) + dsk_ref[...] * xs_ref[rs, :].astype(F32)
        y = y * (z * _sigmoid(z))
        parts = []
        for g in range(N_GROUPS):
            yg = y[:, g * GROUP_W:(g + 1) * GROUP_W]
            ms = jnp.mean(yg * yg, axis=-1, keepdims=True)
            parts.append(yg * lax.rsqrt(ms + EPS))
        yn = jnp.concatenate(parts, axis=1) * nrm_ref[...]
        y_ssd.append(_dot(yn.astype(BF16), wso_ref[...]))
    for rs, (_, graw), y_s in zip(subs, proj, y_ssd):
        m = _sigmoid(graw[:, :D_MODEL]) * yc_ref[rs, :].astype(F32) + _sigmoid(graw[:, D_MODEL:]) * y_s
        mix = _dot(m.astype(BF16), wo_ref[...])
        o_ref[rs, :] = x_ref[rs, :] + mod_ref[0, k_gate:k_gate + 1, :] * mix


def _merge(x, h, yf, yb, xbc, yc, mods, mod_map, wz, bz, wg, bg, dsk, nrm, wso, wo, *, tm, k_gate):
    t = x.shape[0]
    row = pl.BlockSpec((tm, D_MODEL), lambda i: (i, 0))
    wide = pl.BlockSpec((tm, D_INNER), lambda i: (i, 0))
    return pl.pallas_call(
        functools.partial(_merge_kernel, k_gate=k_gate),
        grid=(t // tm,),
        in_specs=[row, row, wide, wide, wide, row,
                  pl.BlockSpec((1, N_MOD, D_MODEL), mod_map),
                  _const_spec((D_MODEL, D_INNER)), _const_spec((1, D_INNER)),
                  _const_spec((D_MODEL, 2 * D_MODEL)), _const_spec((1, 2 * D_MODEL)),
                  _const_spec((1, D_INNER)), _const_spec((1, D_INNER)),
                  _const_spec((D_INNER, D_MODEL)), _const_spec((D_MODEL, D_MODEL))],
        out_specs=row,
        out_shape=jax.ShapeDtypeStruct((t, D_MODEL), F32),
        compiler_params=_params(), name="mixer_merge",
    )(x, h, yf, yb, xbc, yc, mods, wz, bz.reshape(1, -1), wg, bg.reshape(1, -1), dsk.reshape(1, -1),
      nrm.reshape(1, -1), wso, wo)


def kernel(x, c, ctx, c_ctx, w_ada, b_ada, ffn1_norm, ffn1_w_gate, ffn1_w_up, ffn1_w_down, mix_norm, w_in, b_in, conv_dw_w, conv_dw_b, conv_ln_g, conv_ln_b, conv_w_out, ssd_conv_w, ssd_conv_b, ssd_dt_bias, ssd_a_log, ssd_d, ssd_norm, ssd_w_out, w_out, ffn2_norm, ffn2_w_gate, ffn2_w_up, ffn2_w_down, final_norm):
    bsz, seq, _ = x.shape
    ctx_len = ctx.shape[1]
    depth = w_ada.shape[0]
    assert seq % 1024 == 0 and ctx_len % 256 == 0 and bsz + 1 <= 8

    tm_x, tm_c = 512, 256
    xf = x.reshape(bsz * seq, D_MODEL)
    cf = ctx.reshape(bsz * ctx_len, D_MODEL)
    x_map = lambda i: ((i * tm_x) // seq, 0, 0)
    c_map = lambda i: (bsz, 0, 0)

    cvec = jnp.zeros((8, D_MODEL), F32).at[:bsz].set(c).at[bsz].set(c_ctx)
    mods_all = _mods(cvec, w_ada, b_ada)[:, :bsz + 1].reshape(depth, bsz + 1, N_MOD, D_MODEL)

    s1 = 2 * CONV_DIM
    s2 = s1 + D_INNER
    s3 = s2 + XBC_DIM
    s4 = s3 + N_DIR * N_HEADS
    zero_state = jnp.zeros((bsz, N_GROUPS, D_STATE, GROUP_W), F32)

    for i in range(depth):
        last = i == depth - 1
        mods = mods_all[i]
        bf = lambda w: w.astype(BF16)
        w1 = (bf(ffn1_w_gate[i]), bf(ffn1_w_up[i]), bf(ffn1_w_down[i]))
        w2 = (bf(ffn2_w_gate[i]), bf(ffn2_w_up[i]), bf(ffn2_w_down[i]))
        wi, bi = w_in[i], b_in[i]
        wa, wb = bf(wi[:, :CONV_DIM]), bf(wi[:, CONV_DIM:s1])
        ba, bb = bi[:CONV_DIM], bi[CONV_DIM:s1]
        wz, bz = bf(wi[:, s1:s2]), bi[s1:s2]
        wx, bx = bf(wi[:, s2:s3]), bi[s2:s3]
        n_dt = N_DIR * N_HEADS
        wdt = bf(jnp.zeros((D_MODEL, LANES), F32).at[:, :n_dt].set(wi[:, s3:s4]))
        pad_dt = lambda v: jnp.zeros((LANES,), F32).at[:n_dt].set(v.reshape(-1))
        bdt = pad_dt(bi[s3:s4] + ssd_dt_bias[i].reshape(-1))
        a128 = pad_dt(-jnp.exp(ssd_a_log[i].astype(F32))).reshape(1, LANES)
        wgt, bgt = bf(wi[:, s4:]), bi[s4:]
        dsk = jnp.repeat(ssd_d[i], HEAD_DIM)
        conv_w = (wa, wb, ba, bb, conv_dw_w[i], conv_dw_b[i], conv_ln_g[i], conv_ln_b[i], bf(conv_w_out[i]))
        merge_w = (wz, bz, wgt, bgt, dsk, ssd_norm[i], bf(ssd_w_out[i]), bf(w_out[i]))
        feat_w = (wx, bx, wdt, bdt, ssd_conv_w[i], ssd_conv_b[i])

        xf, hx = _ffn(xf, mods, x_map, ffn1_norm[i], *w1, tm=tm_x, k0=0, post="mod", g2=mix_norm[i], k1=3)
        cf, hc = _ffn(cf, mods, c_map, ffn1_norm[i], *w1, tm=tm_c, k0=0, post="mod", g2=mix_norm[i], k1=3)

        xbc_c, dt_c = _ssd_features(hc, *feat_w, tm=tm_c, lseq=ctx_len)
        xbc_x, dt_x = _ssd_features(hx, *feat_w, tm=tm_x, lseq=seq)
        scan_c = functools.partial(_ssd_scan, xbc_c, dt_c, a128, bsz=bsz, lseq=ctx_len, tb=256)
        scan_x = functools.partial(_ssd_scan, xbc_x, dt_x, a128, bsz=bsz, lseq=seq, tb=1024)
        yf_c, s_f = scan_c(zero_state, reverse=False, d=0)
        yb_c, s_b = scan_c(zero_state, reverse=True, d=1)
        yf_x, _ = scan_x(s_f, reverse=False, d=0)
        yb_x, _ = scan_x(s_b, reverse=True, d=1)

        yc_x = _conv_branch(hx, *conv_w, tm=tm_x, seg=GRID_W)
        xf = _merge(xf, hx, yf_x, yb_x, xbc_x, yc_x, mods, x_map, *merge_w, tm=tm_x, k_gate=5)
        if not last:
            yc_c = _conv_branch(hc, *conv_w, tm=tm_c, seg=ctx_len)
            cf = _merge(cf, hc, yf_c, yb_c, xbc_c, yc_c, mods, c_map, *merge_w, tm=tm_c, k_gate=5)

        if last:
            xf = _ffn(xf, mods, x_map, ffn2_norm[i], *w2, tm=tm_x, k0=6, post="final", g2=final_norm)
        else:
            xf = _ffn(xf, mods, x_map, ffn2_norm[i], *w2, tm=tm_x, k0=6)
            cf = _ffn(cf, mods, c_map, ffn2_norm[i], *w2, tm=tm_c, k0=6)
    return xf.reshape(bsz, seq, D_MODEL)
```

```python
import functools

import jax
import jax.numpy as jnp
import numpy as np
from jax import lax
from jax.experimental import pallas as pl
from jax.experimental.pallas import tpu as pltpu

F32 = jnp.float32
BF16 = jnp.bfloat16

D_MODEL = 1024
N_MOD = 9
D_FF = 2816
CONV_DIM = D_MODEL
CONV_K = 31
D_INNER = 2 * D_MODEL
HEAD_DIM = 64
N_HEADS = D_INNER // HEAD_DIM
N_GROUPS = 8
HEADS_PER_GROUP = N_HEADS // N_GROUPS
GROUP_W = HEADS_PER_GROUP * HEAD_DIM
D_STATE = 128
SSD_CONV_K = 5
CHUNK = 128
N_DIR = 2
XBC_DIM = D_INNER + 2 * N_GROUPS * D_STATE
GRID_W = 64
EPS = 1e-6

LANES = 128
BF16_ROWS = 16
FF_CHUNK = 256
VMEM_LIMIT = 58 * 1024 * 1024


def _dot(a, b):
    return jnp.dot(a, b, preferred_element_type=F32)


LOG2E = 1.4426950408889634


def _sigmoid(v):
    return 1.0 / (1.0 + jnp.exp2(v * (-LOG2E)))


def _const_spec(shape):
    nd = len(shape)
    return pl.BlockSpec(shape, lambda *_: (0,) * nd, pipeline_mode=pl.Buffered(1))


def _params(n_axes=1, flags=None):
    return pltpu.CompilerParams(dimension_semantics=("arbitrary",) * n_axes,
                                vmem_limit_bytes=VMEM_LIMIT, flags=flags)


def _mods_kernel(c_ref, w_ref, b_ref, o_ref):
    c = c_ref[...]
    act = c * _sigmoid(c)
    o_ref[0] = jnp.dot(act, w_ref[0], precision=lax.Precision.HIGHEST,
                       preferred_element_type=F32) + b_ref[0]


def _mods(cvec, w_ada, b_ada):
    depth, _, n_out = w_ada.shape
    tn = D_MODEL
    return pl.pallas_call(
        _mods_kernel,
        grid=(depth, n_out // tn),
        in_specs=[pl.BlockSpec((8, D_MODEL), lambda l, j: (0, 0)),
                  pl.BlockSpec((1, D_MODEL, tn), lambda l, j: (l, 0, j)),
                  pl.BlockSpec((1, 1, tn), lambda l, j: (l, 0, j))],
        out_specs=pl.BlockSpec((1, 8, tn), lambda l, j: (l, 0, j)),
        out_shape=jax.ShapeDtypeStruct((depth, 8, n_out), F32),
        compiler_params=_params(2),
        name="ada_mods",
    )(cvec, w_ada, b_ada.reshape(depth, 1, n_out))


def _rms(v, g):
    ms = jnp.mean(v * v, axis=-1, keepdims=True)
    return v * lax.rsqrt(ms + EPS) * g


FFN_SUB = 512


def _ffn_kernel(*refs, k0, post, k1):
    if post == "mod":
        x_ref, mod_ref, g_ref, wg_ref, wu_ref, wd_ref, g2_ref, o_ref, h_ref = refs
    elif post == "final":
        x_ref, mod_ref, g_ref, wg_ref, wu_ref, wd_ref, g2_ref, o_ref = refs
    else:
        x_ref, mod_ref, g_ref, wg_ref, wu_ref, wd_ref, o_ref = refs
    tm = x_ref.shape[0]
    shift = mod_ref[0, k0:k0 + 1, :]
    scale = mod_ref[0, k0 + 1:k0 + 2, :]
    gate = mod_ref[0, k0 + 2:k0 + 3, :]
    n_chunks = D_FF // FF_CHUNK
    subs = [slice(r, r + min(tm, FFN_SUB)) for r in range(0, tm, FFN_SUB)]
    hbs = [(_rms(x_ref[rs, :], g_ref[...]) * (1.0 + scale) + shift).astype(BF16) for rs in subs]

    def gate_up(hb, c):
        sl = slice(c * FF_CHUNK, (c + 1) * FF_CHUNK)
        return _dot(hb, wg_ref[:, sl]), _dot(hb, wu_ref[:, sl])

    accs = [jnp.zeros(hb.shape, F32) for hb in hbs]
    pending = [gate_up(hb, 0) for hb in hbs]
    for c in range(n_chunks):
        for s, hb in enumerate(hbs):
            g, u = pending[s]
            if c + 1 < n_chunks:
                pending[s] = gate_up(hb, c + 1)
            a = (g * _sigmoid(g)) * u
            accs[s] = accs[s] + _dot(a.astype(BF16), wd_ref[c * FF_CHUNK:(c + 1) * FF_CHUNK, :])
    for rs, acc in zip(subs, accs):
        y = x_ref[rs, :] + 0.5 * gate * acc
        if post == "final":
            o_ref[rs, :] = _rms(y, g2_ref[...])
            continue
        o_ref[rs, :] = y
        if post == "mod":
            shift2 = mod_ref[0, k1:k1 + 1, :]
            scale2 = mod_ref[0, k1 + 1:k1 + 2, :]
            h_ref[rs, :] = (_rms(y, g2_ref[...]) * (1.0 + scale2) + shift2).astype(BF16)


def _ffn(x, mods, mod_map, norm_g, wg, wu, wd, *, tm, k0, post="none", g2=None, k1=0):
    t = x.shape[0]
    row = pl.BlockSpec((tm, D_MODEL), lambda i: (i, 0))
    in_specs = [row,
                pl.BlockSpec((1, N_MOD, D_MODEL), mod_map),
                _const_spec((1, D_MODEL)),
                _const_spec((D_MODEL, D_FF)), _const_spec((D_MODEL, D_FF)),
                _const_spec((D_FF, D_MODEL))]
    args = [x, mods, norm_g.reshape(1, D_MODEL), wg, wu, wd]
    if post != "none":
        in_specs.append(_const_spec((1, D_MODEL)))
        args.append(g2.reshape(1, D_MODEL))
    if post == "mod":
        out_specs = [row, row]
        out_shape = [jax.ShapeDtypeStruct((t, D_MODEL), F32), jax.ShapeDtypeStruct((t, D_MODEL), BF16)]
    else:
        out_specs = row
        out_shape = jax.ShapeDtypeStruct((t, D_MODEL), F32)
    return pl.pallas_call(
        functools.partial(_ffn_kernel, k0=k0, post=post, k1=k1),
        grid=(t // tm,), in_specs=in_specs, out_specs=out_specs, out_shape=out_shape,
        compiler_params=_params(), name="half_ffn_" + post,
    )(*args)


CONV_K_PAD = 32
CONV_SUB = 256


def _split_bf16(m):
    hi = jnp.asarray(m, F32).astype(BF16)
    lo = (jnp.asarray(m, F32) - hi.astype(F32)).astype(BF16)
    return hi, lo


def _dft_consts(seg):
    n = 2 * seg
    t = np.arange(seg)
    r = np.arange(seg)
    th = 2.0 * np.pi * np.outer(r, t) / n
    fwd = np.concatenate([np.cos(th), np.sin(th)], axis=0)
    fwd[seg] = np.cos(np.pi * t)
    inv = np.concatenate([2.0 * np.cos(th.T), 2.0 * np.sin(th.T)], axis=1) / n
    inv[:, 0] = 1.0 / n
    inv[:, seg] = np.cos(np.pi * t) / n
    shift = CONV_K // 2 - np.arange(CONV_K)
    ph = 2.0 * np.pi * np.outer(r, shift) / n
    re, im = np.cos(ph), -np.sin(ph)
    re_nyq = re.copy()
    re_nyq[0] = np.cos(np.pi * shift)
    taps = np.zeros((3 * seg, CONV_K_PAD))
    taps[:, :CONV_K] = np.concatenate([re, im, re_nyq], axis=0)
    f2 = jnp.concatenate(_split_bf16(fwd), axis=1)
    g2 = jnp.concatenate(_split_bf16(inv), axis=1)
    return f2, g2, jnp.asarray(taps, F32)


def _conv_kernel(h_ref, wa_ref, wb_ref, ba_ref, bb_ref, dww_ref, dwb_ref, lng_ref, lnb_ref, wo_ref,
                 f2_ref, g2_ref, taps_ref, o_ref, coef_ref, conv_ref, *, seg):
    tm = h_ref.shape[0]

    @pl.when(pl.program_id(0) == 0)
    def _():
        coef_ref[...] = jnp.dot(taps_ref[...], dww_ref[...], precision=lax.Precision.HIGHEST,
                                preferred_element_type=F32)

    sub = max(seg, min(tm, CONV_SUB))
    subs = [slice(r, r + sub) for r in range(0, tm, sub)]
    glu = []
    for rs in subs:
        hb = h_ref[rs, :]
        glu.append((_dot(hb, wa_ref[...]) + ba_ref[...], _dot(hb, wb_ref[...]) + bb_ref[...]))
    specs = []
    for a, b in glu:
        ub = (a * _sigmoid(b)).astype(BF16)
        for s in range(sub // seg):
            us = ub[s * seg:(s + 1) * seg, :]
            specs.append(_dot(f2_ref[...], jnp.concatenate([us, us], axis=0)))
    h_re = coef_ref[0:seg, :]
    h_im = coef_ref[seg:2 * seg, :]
    h_re_nyq = coef_ref[2 * seg:3 * seg, :]
    for s, spec in enumerate(specs):
        co, si = spec[:seg, :], spec[seg:, :]
        y = jnp.concatenate([co * h_re + si * h_im, si * h_re_nyq - co * h_im], axis=0).astype(BF16)
        conv_ref[s * seg:(s + 1) * seg, :] = _dot(g2_ref[...], jnp.concatenate([y, y], axis=0)) + dwb_ref[...]

    for rs in subs:
        v = conv_ref[rs, :]
        mu = jnp.mean(v, axis=-1, keepdims=True)
        vc = v - mu
        var = jnp.mean(vc * vc, axis=-1, keepdims=True)
        vn = vc * lax.rsqrt(var + EPS) * lng_ref[...] + lnb_ref[...]
        act = vn * _sigmoid(vn)
        o_ref[rs, :] = _dot(act.astype(BF16), wo_ref[...]).astype(BF16)


def _conv_branch(h, wa, wb, ba, bb, dww, dwb, lng, lnb, wo, *, tm, seg):
    t = h.shape[0]
    assert 2 * seg >= seg + CONV_K - 1
    f2, g2, taps = _dft_consts(seg)
    dww_pad = jnp.zeros((CONV_K_PAD, CONV_DIM), F32).at[:CONV_K].set(dww)
    row = pl.BlockSpec((tm, D_MODEL), lambda i: (i, 0))
    vec = _const_spec((1, CONV_DIM))
    sq = _const_spec((D_MODEL, CONV_DIM))
    return pl.pallas_call(
        functools.partial(_conv_kernel, seg=seg),
        grid=(t // tm,),
        in_specs=[row, sq, sq, vec, vec, _const_spec((CONV_K_PAD, CONV_DIM)), vec, vec, vec, sq,
                  _const_spec(f2.shape), _const_spec(g2.shape), _const_spec(taps.shape)],
        out_specs=row,
        out_shape=jax.ShapeDtypeStruct((t, D_MODEL), BF16),
        scratch_shapes=[pltpu.VMEM((3 * seg, CONV_DIM), F32),
                        pltpu.VMEM((tm, CONV_DIM), F32)],
        compiler_params=_params(), name="conv_branch",
    )(h, wa, wb, ba.reshape(1, -1), bb.reshape(1, -1), dww_pad, dwb.reshape(1, -1), lng.reshape(1, -1),
      lnb.reshape(1, -1), wo, f2, g2, taps)


FEAT_ROWS = 64
FEAT_COLS = 512
FEAT_STRIDE = 9
SUBLANES = 8
FEAT_SPAN = SUBLANES * FEAT_STRIDE


def _aligned(v, m):
    return v if isinstance(v, int) else pl.multiple_of(v, m)


def _feat_kernel(hp_ref, h_ref, hn_ref, wx_ref, bx_ref, wdt_ref, bdt_ref, cw_ref, cb_ref,
                 xbc_ref, dt_ref, hfull_ref, raw0_ref, raw1_ref, act0_ref, act1_ref, *, lseq):
    tm = h_ref.shape[0]
    halo = BF16_ROWS
    i = pl.program_id(0)
    hfull_ref[0:halo, :] = hp_ref[...]
    hfull_ref[halo:halo + tm, :] = h_ref[...]
    hfull_ref[halo + tm:, :] = hn_ref[...]
    t0 = i * tm
    seq_lo = (t0 // lseq) * lseq
    halo_iota = lax.broadcasted_iota(jnp.int32, (halo, 1), 0)
    keep_lo = (t0 - halo + halo_iota) >= seq_lo
    keep_hi = (t0 + tm + halo_iota) < seq_lo + lseq
    first = halo - SSD_CONV_K // 2
    n_lane_blk = FEAT_COLS // LANES
    n_span = tm // FEAT_SPAN

    def project(j, raw_ref):
        cols = pl.ds(_aligned(j * FEAT_COLS, FEAT_COLS), FEAT_COLS)
        raw = _dot(hfull_ref[...], wx_ref[:, cols]) + bx_ref[:, cols]
        for lb in range(n_lane_blk):
            part = raw[:, lb * LANES:(lb + 1) * LANES]
            raw_ref[lb, 0:halo, :] = jnp.where(keep_lo, part[0:halo, :], 0.0)
            raw_ref[lb, halo:halo + tm, :] = part[halo:halo + tm, :]
            raw_ref[lb, halo + tm:, :] = jnp.where(keep_hi, part[halo + tm:, :], 0.0)

    def conv(j, raw_ref, act_ref):
        for lb in range(n_lane_blk):
            dst = pl.ds(_aligned(j * FEAT_COLS + lb * LANES, LANES), LANES)
            taps = [cw_ref[k:k + 1, dst] for k in range(SSD_CONV_K)]
            bias = cb_ref[:, dst]
            for sp in range(n_span):
                base = sp * FEAT_SPAN
                srcs = [raw_ref[lb, pl.ds(base + first + o, SUBLANES, stride=FEAT_STRIDE), :]
                        for o in range(FEAT_STRIDE + SSD_CONV_K - 1)]
                for r in range(FEAT_STRIDE):
                    acc = bias
                    for k in range(SSD_CONV_K):
                        acc = acc + srcs[r + k] * taps[k]
                    act_ref[lb, pl.ds(base + r, SUBLANES, stride=FEAT_STRIDE), :] = acc * _sigmoid(acc)
            for r0 in range(n_span * FEAT_SPAN, tm, FEAT_ROWS):
                nr = min(FEAT_ROWS, tm - r0)
                acc = bias
                for k in range(SSD_CONV_K):
                    acc = acc + raw_ref[lb, r0 + first + k:r0 + first + k + nr, :] * taps[k]
                act_ref[lb, r0:r0 + nr, :] = acc * _sigmoid(acc)
            xbc_ref[:, dst] = act_ref[lb].astype(BF16)

    n_blk = XBC_DIM // FEAT_COLS
    project(0, raw0_ref)

    def body(jj, carry):
        j = 2 * jj
        project(j + 1, raw1_ref)
        conv(j, raw0_ref, act0_ref)
        project(j + 2, raw0_ref)
        conv(j + 1, raw1_ref, act1_ref)
        return carry

    lax.fori_loop(0, (n_blk - 2) // 2, body, 0)
    project(n_blk - 1, raw1_ref)
    conv(n_blk - 2, raw0_ref, act0_ref)
    conv(n_blk - 1, raw1_ref, act1_ref)

    dtr = _dot(h_ref[...], wdt_ref[...]) + bdt_ref[...]
    sp = jnp.maximum(dtr, 0.0) + jnp.log1p(jnp.exp(-jnp.abs(dtr)))
    lane = lax.broadcasted_iota(jnp.int32, (1, LANES), 1)
    dt_ref[...] = jnp.where(lane < N_DIR * N_HEADS, sp, 0.0)


def _ssd_features(h, wx, bx, wdt, bdt, cw, cb, *, tm, lseq):
    t = h.shape[0]
    nb = tm // BF16_ROWS
    last = t // BF16_ROWS - 1
    return pl.pallas_call(
        functools.partial(_feat_kernel, lseq=lseq),
        grid=(t // tm,),
        in_specs=[pl.BlockSpec((BF16_ROWS, D_MODEL), lambda i: (jnp.maximum(i * nb - 1, 0), 0)),
                  pl.BlockSpec((tm, D_MODEL), lambda i: (i, 0)),
                  pl.BlockSpec((BF16_ROWS, D_MODEL), lambda i: (jnp.minimum((i + 1) * nb, last), 0)),
                  _const_spec((D_MODEL, XBC_DIM)), _const_spec((1, XBC_DIM)),
                  _const_spec((D_MODEL, LANES)), _const_spec((1, LANES)),
                  _const_spec((SSD_CONV_K, XBC_DIM)), _const_spec((1, XBC_DIM))],
        out_specs=[pl.BlockSpec((tm, XBC_DIM), lambda i: (i, 0)),
                   pl.BlockSpec((tm, LANES), lambda i: (i, 0))],
        out_shape=[jax.ShapeDtypeStruct((t, XBC_DIM), BF16), jax.ShapeDtypeStruct((t, LANES), F32)],
        scratch_shapes=[pltpu.VMEM((tm + 2 * BF16_ROWS, D_MODEL), BF16),
                        pltpu.VMEM((FEAT_COLS // LANES, tm + 2 * BF16_ROWS, LANES), F32),
                        pltpu.VMEM((FEAT_COLS // LANES, tm + 2 * BF16_ROWS, LANES), F32),
                        pltpu.VMEM((FEAT_COLS // LANES, tm, LANES), F32),
                        pltpu.VMEM((FEAT_COLS // LANES, tm, LANES), F32)],
        compiler_params=_params(), name="ssd_features",
    )(h, h, h, wx, bx.reshape(1, -1), wdt, bdt.reshape(1, -1), cw, cb.reshape(1, -1))


def _scan_kernel(xs_ref, bm_ref, cm_ref, dt_ref, a_ref, s0_ref, y_ref, sf_ref, state_ref, *, reverse, d):
    tb = xs_ref.shape[0]
    nch = tb // CHUNK
    j = pl.program_id(1)

    @pl.when(j == 0)
    def _():
        state_ref[...] = s0_ref[0]

    row_i = lax.broadcasted_iota(jnp.int32, (CHUNK, CHUNK), 0)
    col_i = lax.broadcasted_iota(jnp.int32, (CHUNK, CHUNK), 1)
    causal = (col_i >= row_i) if reverse else (row_i >= col_i)
    tri = causal.astype(F32)
    edge = 0 if reverse else CHUNK - 1
    lo64 = lax.broadcasted_iota(jnp.int32, (1, LANES), 1) < HEAD_DIM

    def pair(v0, v1):
        return jnp.where(lo64, v0, v1)

    def chunk_rows(c):
        ci = (nch - 1 - c) if reverse else c
        return pl.ds(pl.multiple_of(ci * CHUNK, CHUNK), CHUNK)

    def decay_terms(c):
        dt = dt_ref[chunk_rows(c), :]
        acum = jnp.dot(tri, dt * (a_ref[...] * LOG2E), precision=lax.Precision.HIGHEST,
                       preferred_element_type=F32)
        acum_t = acum.T
        dt_t = dt.T
        a_edge_t = jnp.broadcast_to(acum_t[:, edge:edge + 1], (LANES, CHUNK))
        w_t = (jnp.exp2(a_edge_t - acum_t) * dt_t).astype(BF16)
        return acum, acum_t, w_t, dt_t.astype(BF16)

    def chunk_body(c, terms):
        rows = chunk_rows(c)
        acum, acum_t, w_t, dt_tb = terms
        next_terms = decay_terms(jnp.minimum(c + 1, nch - 1))
        zeros_half = jnp.zeros((CHUNK, LANES), BF16)
        staged = []
        for g in range(N_GROUPS):
            bg = bm_ref[rows, g * D_STATE:(g + 1) * D_STATE]
            cg = cm_ref[rows, g * D_STATE:(g + 1) * D_STATE]
            bg_t = bg.astype(F32).T.astype(BF16)
            cb = jnp.where(causal, _dot(cg, bg_t), 0.0).astype(BF16)
            s_in = state_ref[g]
            y_off = _dot(cg, s_in.astype(BF16))
            m_parts, bw_parts, a_cols, rhs_parts = [], [], [], []
            for hh in range(HEADS_PER_GROUP):
                q = d * N_HEADS + g * HEADS_PER_GROUP + hh
                a_col = jnp.broadcast_to(acum[:, q:q + 1], (CHUNK, CHUNK))
                decay = jnp.exp2(jnp.minimum(a_col - acum_t[q:q + 1, :], 0.0))
                m_parts.append(cb * decay.astype(BF16) * dt_tb[q:q + 1, :])
                bw_parts.append(bg_t * w_t[q:q + 1, :])
                a_cols.append(a_col)
                half = xs_ref[rows, g * GROUP_W + (hh // 2) * LANES:g * GROUP_W + (hh // 2 + 1) * LANES]
                own = jnp.where(lo64 if hh % 2 == 0 else jnp.logical_not(lo64), half, zeros_half)
                rhs_parts.append(jnp.concatenate([own, zeros_half] if hh < 2 else [zeros_half, own], axis=1))
            lhs = jnp.concatenate([jnp.concatenate(m_parts, axis=1), jnp.concatenate(bw_parts, axis=1)], axis=0)
            e_grp = jnp.exp2(jnp.concatenate([pair(a_cols[0], a_cols[1]), pair(a_cols[2], a_cols[3])], axis=1))
            staged.append((lhs, jnp.concatenate(rhs_parts, axis=0), s_in, y_off, e_grp))
        for g, (lhs, rhs, s_in, y_off, e_grp) in enumerate(staged):
            res = _dot(lhs, rhs)
            y = res[:CHUNK, :] + y_off * e_grp
            y_ref[rows, g * GROUP_W:(g + 1) * GROUP_W] = y.astype(BF16)
            state_ref[g] = s_in * e_grp[edge:edge + 1, :] + res[CHUNK:, :]
        return next_terms

    lax.fori_loop(0, nch, chunk_body, decay_terms(0))

    @pl.when(j == pl.num_programs(1) - 1)
    def _():
        sf_ref[0] = state_ref[...]


def _ssd_scan(xbc, dt, a128, s0, *, bsz, lseq, tb, reverse, d):
    nblk = lseq // tb

    def blk(b, j):
        return b * nblk + ((nblk - 1 - j) if reverse else j)

    n_xs = D_INNER // (N_GROUPS * D_STATE)
    st_spec = pl.BlockSpec((1, N_GROUPS, D_STATE, GROUP_W), lambda b, j: (b, 0, 0, 0))
    return pl.pallas_call(
        functools.partial(_scan_kernel, reverse=reverse, d=d),
        grid=(bsz, nblk),
        in_specs=[pl.BlockSpec((tb, D_INNER), lambda b, j: (blk(b, j), 0)),
                  pl.BlockSpec((tb, N_GROUPS * D_STATE), lambda b, j: (blk(b, j), n_xs)),
                  pl.BlockSpec((tb, N_GROUPS * D_STATE), lambda b, j: (blk(b, j), n_xs + 1)),
                  pl.BlockSpec((tb, LANES), lambda b, j: (blk(b, j), 0)),
                  pl.BlockSpec((1, LANES), lambda b, j: (0, 0)),
                  st_spec],
        out_specs=[pl.BlockSpec((tb, D_INNER), lambda b, j: (blk(b, j), 0)), st_spec],
        out_shape=[jax.ShapeDtypeStruct((bsz * lseq, D_INNER), BF16),
                   jax.ShapeDtypeStruct((bsz, N_GROUPS, D_STATE, GROUP_W), F32)],
        scratch_shapes=[pltpu.VMEM((N_GROUPS, D_STATE, GROUP_W), F32)],
        compiler_params=_params(2), name="ssd_scan_bwd" if reverse else "ssd_scan_fwd",
    )(xbc, xbc, xbc, dt, a128, s0)


MERGE_SUB = 256


def _merge_kernel(x_ref, h_ref, yf_ref, yb_ref, xs_ref, yc_ref, mod_ref, wz_ref, bz_ref, wg_ref, bg_ref,
                  dsk_ref, nrm_ref, wso_ref, wo_ref, o_ref, *, k_gate):
    tm = x_ref.shape[0]
    subs = [slice(r, r + MERGE_SUB) for r in range(0, tm, MERGE_SUB)]
    proj = []
    for rs in subs:
        hb = h_ref[rs, :]
        proj.append((_dot(hb, wz_ref[...]) + bz_ref[...], _dot(hb, wg_ref[...]) + bg_ref[...]))
    y_ssd = []
    for rs, (z, _) in zip(subs, proj):
        y = yf_ref[rs, :].astype(F32) + yb_ref[rs, :].astype(F32) + dsk_ref[...] * xs_ref[rs, :].astype(F32)
        y = y * (z * _sigmoid(z))
        parts = []
        for g in range(N_GROUPS):
            yg = y[:, g * GROUP_W:(g + 1) * GROUP_W]
            ms = jnp.mean(yg * yg, axis=-1, keepdims=True)
            parts.append(yg * lax.rsqrt(ms + EPS))
        yn = jnp.concatenate(parts, axis=1) * nrm_ref[...]
        y_ssd.append(_dot(yn.astype(BF16), wso_ref[...]))
    for rs, (_, graw), y_s in zip(subs, proj, y_ssd):
        m = _sigmoid(graw[:, :D_MODEL]) * yc_ref[rs, :].astype(F32) + _sigmoid(graw[:, D_MODEL:]) * y_s
        mix = _dot(m.astype(BF16), wo_ref[...])
        o_ref[rs, :] = x_ref[rs, :] + mod_ref[0, k_gate:k_gate + 1, :] * mix


def _merge(x, h, yf, yb, xbc, yc, mods, mod_map, wz, bz, wg, bg, dsk, nrm, wso, wo, *, tm, k_gate):
    t = x.shape[0]
    row = pl.BlockSpec((tm, D_MODEL), lambda i: (i, 0))
    wide = pl.BlockSpec((tm, D_INNER), lambda i: (i, 0))
    return pl.pallas_call(
        functools.partial(_merge_kernel, k_gate=k_gate),
        grid=(t // tm,),
        in_specs=[row, row, wide, wide, wide, row,
                  pl.BlockSpec((1, N_MOD, D_MODEL), mod_map),
                  _const_spec((D_MODEL, D_INNER)), _const_spec((1, D_INNER)),
                  _const_spec((D_MODEL, 2 * D_MODEL)), _const_spec((1, 2 * D_MODEL)),
                  _const_spec((1, D_INNER)), _const_spec((1, D_INNER)),
                  _const_spec((D_INNER, D_MODEL)), _const_spec((D_MODEL, D_MODEL))],
        out_specs=row,
        out_shape=jax.ShapeDtypeStruct((t, D_MODEL), F32),
        compiler_params=_params(), name="mixer_merge",
    )(x, h, yf, yb, xbc, yc, mods, wz, bz.reshape(1, -1), wg, bg.reshape(1, -1), dsk.reshape(1, -1),
      nrm.reshape(1, -1), wso, wo)


def kernel(x, c, ctx, c_ctx, w_ada, b_ada, ffn1_norm, ffn1_w_gate, ffn1_w_up, ffn1_w_down, mix_norm, w_in, b_in, conv_dw_w, conv_dw_b, conv_ln_g, conv_ln_b, conv_w_out, ssd_conv_w, ssd_conv_b, ssd_dt_bias, ssd_a_log, ssd_d, ssd_norm, ssd_w_out, w_out, ffn2_norm, ffn2_w_gate, ffn2_w_up, ffn2_w_down, final_norm):
    bsz, seq, _ = x.shape
    ctx_len = ctx.shape[1]
    depth = w_ada.shape[0]
    assert seq % 1024 == 0 and ctx_len % 256 == 0 and bsz + 1 <= 8

    tm_x, tm_c = 512, 256
    xf = x.reshape(bsz * seq, D_MODEL)
    cf = ctx.reshape(bsz * ctx_len, D_MODEL)
    tm_ffn = 2 * tm_x
    tm_ca = bsz * ctx_len
    assert tm_ca % MERGE_SUB == 0 and tm_ca <= tm_ffn
    x_map = lambda i: ((i * tm_x) // seq, 0, 0)
    x_map_ffn = lambda i: ((i * tm_ffn) // seq, 0, 0)
    c_map = lambda i: (bsz, 0, 0)

    cvec = jnp.zeros((8, D_MODEL), F32).at[:bsz].set(c).at[bsz].set(c_ctx)
    mods_all = _mods(cvec, w_ada, b_ada)[:, :bsz + 1].reshape(depth, bsz + 1, N_MOD, D_MODEL)

    s1 = 2 * CONV_DIM
    s2 = s1 + D_INNER
    s3 = s2 + XBC_DIM
    s4 = s3 + N_DIR * N_HEADS
    zero_state = jnp.zeros((bsz, N_GROUPS, D_STATE, GROUP_W), F32)

    for i in range(depth):
        last = i == depth - 1
        mods = mods_all[i]
        bf = lambda w: w.astype(BF16)
        w1 = (bf(ffn1_w_gate[i]), bf(ffn1_w_up[i]), bf(ffn1_w_down[i]))
        w2 = (bf(ffn2_w_gate[i]), bf(ffn2_w_up[i]), bf(ffn2_w_down[i]))
        wi, bi = w_in[i], b_in[i]
        wa, wb = bf(wi[:, :CONV_DIM]), bf(wi[:, CONV_DIM:s1])
        ba, bb = bi[:CONV_DIM], bi[CONV_DIM:s1]
        wz, bz = bf(wi[:, s1:s2]), bi[s1:s2]
        wx, bx = bf(wi[:, s2:s3]), bi[s2:s3]
        n_dt = N_DIR * N_HEADS
        wdt = bf(jnp.zeros((D_MODEL, LANES), F32).at[:, :n_dt].set(wi[:, s3:s4]))
        pad_dt = lambda v: jnp.zeros((LANES,), F32).at[:n_dt].set(v.reshape(-1))
        bdt = pad_dt(bi[s3:s4] + ssd_dt_bias[i].reshape(-1))
        a128 = pad_dt(-jnp.exp(ssd_a_log[i].astype(F32))).reshape(1, LANES)
        wgt, bgt = bf(wi[:, s4:]), bi[s4:]
        dsk = jnp.repeat(ssd_d[i], HEAD_DIM)
        conv_w = (wa, wb, ba, bb, conv_dw_w[i], conv_dw_b[i], conv_ln_g[i], conv_ln_b[i], bf(conv_w_out[i]))
        merge_w = (wz, bz, wgt, bgt, dsk, ssd_norm[i], bf(ssd_w_out[i]), bf(w_out[i]))
        feat_w = (wx, bx, wdt, bdt, ssd_conv_w[i], ssd_conv_b[i])

        xf, hx = _ffn(xf, mods, x_map_ffn, ffn1_norm[i], *w1, tm=tm_ffn, k0=0, post="mod", g2=mix_norm[i], k1=3)
        cf, hc = _ffn(cf, mods, c_map, ffn1_norm[i], *w1, tm=tm_ca, k0=0, post="mod", g2=mix_norm[i], k1=3)

        xbc_c, dt_c = _ssd_features(hc, *feat_w, tm=tm_c, lseq=ctx_len)
        xbc_x, dt_x = _ssd_features(hx, *feat_w, tm=tm_x, lseq=seq)
        scan_c = functools.partial(_ssd_scan, xbc_c, dt_c, a128, bsz=bsz, lseq=ctx_len, tb=256)
        scan_x = functools.partial(_ssd_scan, xbc_x, dt_x, a128, bsz=bsz, lseq=seq, tb=1024)
        yf_c, s_f = scan_c(zero_state, reverse=False, d=0)
        yb_c, s_b = scan_c(zero_state, reverse=True, d=1)
        yf_x, _ = scan_x(s_f, reverse=False, d=0)
        yb_x, _ = scan_x(s_b, reverse=True, d=1)

        yc_x = _conv_branch(hx, *conv_w, tm=tm_x, seg=GRID_W)
        xf = _merge(xf, hx, yf_x, yb_x, xbc_x, yc_x, mods, x_map, *merge_w, tm=tm_x, k_gate=5)
        if not last:
            yc_c = _conv_branch(hc, *conv_w, tm=tm_ca, seg=ctx_len)
            cf = _merge(cf, hc, yf_c, yb_c, xbc_c, yc_c, mods, c_map, *merge_w, tm=tm_ca, k_gate=5)

        if last:
            xf = _ffn(xf, mods, x_map_ffn, ffn2_norm[i], *w2, tm=tm_ffn, k0=6, post="final", g2=final_norm)
        else:
            xf = _ffn(xf, mods, x_map_ffn, ffn2_norm[i], *w2, tm=tm_ffn, k0=6)
            cf = _ffn(cf, mods, c_map, ffn2_norm[i], *w2, tm=tm_ca, k0=6)
    return xf.reshape(bsz, seq, D_MODEL)
```

```python
import functools

import jax
import jax.numpy as jnp
import numpy as np
from jax import lax
from jax.experimental import pallas as pl
from jax.experimental.pallas import tpu as pltpu

F32 = jnp.float32
BF16 = jnp.bfloat16

D_MODEL = 1024
N_MOD = 9
D_FF = 2816
CONV_DIM = D_MODEL
CONV_K = 31
D_INNER = 2 * D_MODEL
HEAD_DIM = 64
N_HEADS = D_INNER // HEAD_DIM
N_GROUPS = 8
HEADS_PER_GROUP = N_HEADS // N_GROUPS
GROUP_W = HEADS_PER_GROUP * HEAD_DIM
D_STATE = 128
SSD_CONV_K = 5
CHUNK = 128
N_DIR = 2
XBC_DIM = D_INNER + 2 * N_GROUPS * D_STATE
GRID_W = 64
EPS = 1e-6

LANES = 128
BF16_ROWS = 16
FF_CHUNK = 256
VMEM_LIMIT = 58 * 1024 * 1024


def _dot(a, b):
    return jnp.dot(a, b, preferred_element_type=F32)


LOG2E = 1.4426950408889634


def _sigmoid(v):
    return 1.0 / (1.0 + jnp.exp2(v * (-LOG2E)))


def _const_spec(shape):
    nd = len(shape)
    return pl.BlockSpec(shape, lambda *_: (0,) * nd, pipeline_mode=pl.Buffered(1))


def _params(n_axes=1, flags=None):
    return pltpu.CompilerParams(dimension_semantics=("arbitrary",) * n_axes,
                                vmem_limit_bytes=VMEM_LIMIT, flags=flags)


def _mods_kernel(c_ref, w_ref, b_ref, o_ref):
    c = c_ref[...]
    act = c * _sigmoid(c)
    o_ref[0] = jnp.dot(act, w_ref[0], precision=lax.Precision.HIGHEST,
                       preferred_element_type=F32) + b_ref[0]


def _mods(cvec, w_ada, b_ada):
    depth, _, n_out = w_ada.shape
    tn = D_MODEL
    return pl.pallas_call(
        _mods_kernel,
        grid=(depth, n_out // tn),
        in_specs=[pl.BlockSpec((8, D_MODEL), lambda l, j: (0, 0)),
                  pl.BlockSpec((1, D_MODEL, tn), lambda l, j: (l, 0, j)),
                  pl.BlockSpec((1, 1, tn), lambda l, j: (l, 0, j))],
        out_specs=pl.BlockSpec((1, 8, tn), lambda l, j: (l, 0, j)),
        out_shape=jax.ShapeDtypeStruct((depth, 8, n_out), F32),
        compiler_params=_params(2),
        name="ada_mods",
    )(cvec, w_ada, b_ada.reshape(depth, 1, n_out))


def _rms(v, g):
    ms = jnp.mean(v * v, axis=-1, keepdims=True)
    return v * lax.rsqrt(ms + EPS) * g


FFN_SUB = 512


def _ffn_kernel(*refs, k0, post, k1):
    if post == "mod":
        x_ref, mod_ref, g_ref, wg_ref, wu_ref, wd_ref, g2_ref, o_ref, h_ref = refs
    elif post == "final":
        x_ref, mod_ref, g_ref, wg_ref, wu_ref, wd_ref, g2_ref, o_ref = refs
    else:
        x_ref, mod_ref, g_ref, wg_ref, wu_ref, wd_ref, o_ref = refs
    tm = x_ref.shape[0]
    shift = mod_ref[0, k0:k0 + 1, :]
    scale = mod_ref[0, k0 + 1:k0 + 2, :]
    gate = mod_ref[0, k0 + 2:k0 + 3, :]
    n_chunks = D_FF // FF_CHUNK
    subs = [slice(r, r + min(tm, FFN_SUB)) for r in range(0, tm, FFN_SUB)]
    hbs = [(_rms(x_ref[rs, :], g_ref[...]) * (1.0 + scale) + shift).astype(BF16) for rs in subs]

    def gate_up(hb, c):
        sl = slice(c * FF_CHUNK, (c + 1) * FF_CHUNK)
        return _dot(hb, wg_ref[:, sl]), _dot(hb, wu_ref[:, sl])

    accs = [jnp.zeros(hb.shape, F32) for hb in hbs]
    pending = [gate_up(hb, 0) for hb in hbs]
    for c in range(n_chunks):
        for s, hb in enumerate(hbs):
            g, u = pending[s]
            if c + 1 < n_chunks:
                pending[s] = gate_up(hb, c + 1)
            a = (g * _sigmoid(g)) * u
            accs[s] = accs[s] + _dot(a.astype(BF16), wd_ref[c * FF_CHUNK:(c + 1) * FF_CHUNK, :])
    for rs, acc in zip(subs, accs):
        y = x_ref[rs, :] + 0.5 * gate * acc
        if post == "final":
            o_ref[rs, :] = _rms(y, g2_ref[...])
            continue
        o_ref[rs, :] = y
        if post == "mod":
            shift2 = mod_ref[0, k1:k1 + 1, :]
            scale2 = mod_ref[0, k1 + 1:k1 + 2, :]
            h_ref[rs, :] = (_rms(y, g2_ref[...]) * (1.0 + scale2) + shift2).astype(BF16)


def _ffn(x, mods, mod_map, norm_g, wg, wu, wd, *, tm, k0, post="none", g2=None, k1=0):
    t = x.shape[0]
    row = pl.BlockSpec((tm, D_MODEL), lambda i: (i, 0))
    in_specs = [row,
                pl.BlockSpec((1, N_MOD, D_MODEL), mod_map),
                _const_spec((1, D_MODEL)),
                _const_spec((D_MODEL, D_FF)), _const_spec((D_MODEL, D_FF)),
                _const_spec((D_FF, D_MODEL))]
    args = [x, mods, norm_g.reshape(1, D_MODEL), wg, wu, wd]
    if post != "none":
        in_specs.append(_const_spec((1, D_MODEL)))
        args.append(g2.reshape(1, D_MODEL))
    if post == "mod":
        out_specs = [row, row]
        out_shape = [jax.ShapeDtypeStruct((t, D_MODEL), F32), jax.ShapeDtypeStruct((t, D_MODEL), BF16)]
    else:
        out_specs = row
        out_shape = jax.ShapeDtypeStruct((t, D_MODEL), F32)
    return pl.pallas_call(
        functools.partial(_ffn_kernel, k0=k0, post=post, k1=k1),
        grid=(t // tm,), in_specs=in_specs, out_specs=out_specs, out_shape=out_shape,
        compiler_params=_params(), name="half_ffn_" + post,
    )(*args)


CONV_K_PAD = 32
CONV_SUB = 256


def _split_bf16(m):
    hi = jnp.asarray(m, F32).astype(BF16)
    lo = (jnp.asarray(m, F32) - hi.astype(F32)).astype(BF16)
    return hi, lo


def _dft_consts(seg):
    n = 2 * seg
    t = np.arange(seg)
    r = np.arange(seg)
    th = 2.0 * np.pi * np.outer(r, t) / n
    fwd = np.concatenate([np.cos(th), np.sin(th)], axis=0)
    fwd[seg] = np.cos(np.pi * t)
    inv = np.concatenate([2.0 * np.cos(th.T), 2.0 * np.sin(th.T)], axis=1) / n
    inv[:, 0] = 1.0 / n
    inv[:, seg] = np.cos(np.pi * t) / n
    shift = CONV_K // 2 - np.arange(CONV_K)
    ph = 2.0 * np.pi * np.outer(r, shift) / n
    re, im = np.cos(ph), -np.sin(ph)
    re_nyq = re.copy()
    re_nyq[0] = np.cos(np.pi * shift)
    taps = np.zeros((3 * seg, CONV_K_PAD))
    taps[:, :CONV_K] = np.concatenate([re, im, re_nyq], axis=0)
    f2 = jnp.concatenate(_split_bf16(fwd), axis=1)
    g2 = jnp.concatenate(_split_bf16(inv), axis=1)
    return f2, g2, jnp.asarray(taps, F32)


def _conv_kernel(h_ref, wa_ref, wb_ref, ba_ref, bb_ref, dww_ref, dwb_ref, lng_ref, lnb_ref, wo_ref,
                 f2_ref, g2_ref, taps_ref, o_ref, coef_ref, conv_ref, *, seg):
    tm = h_ref.shape[0]

    @pl.when(pl.program_id(0) == 0)
    def _():
        coef_ref[...] = jnp.dot(taps_ref[...], dww_ref[...], precision=lax.Precision.HIGHEST,
                                preferred_element_type=F32)

    sub = max(seg, min(tm, CONV_SUB))
    subs = [slice(r, r + sub) for r in range(0, tm, sub)]
    glu = []
    for rs in subs:
        hb = h_ref[rs, :]
        glu.append((_dot(hb, wa_ref[...]) + ba_ref[...], _dot(hb, wb_ref[...]) + bb_ref[...]))
    specs = []
    for a, b in glu:
        ub = (a * _sigmoid(b)).astype(BF16)
        for s in range(sub // seg):
            us = ub[s * seg:(s + 1) * seg, :]
            specs.append(_dot(f2_ref[...], jnp.concatenate([us, us], axis=0)))
    h_re = coef_ref[0:seg, :]
    h_im = coef_ref[seg:2 * seg, :]
    h_re_nyq = coef_ref[2 * seg:3 * seg, :]
    for s, spec in enumerate(specs):
        co, si = spec[:seg, :], spec[seg:, :]
        y = jnp.concatenate([co * h_re + si * h_im, si * h_re_nyq - co * h_im], axis=0).astype(BF16)
        conv_ref[s * seg:(s + 1) * seg, :] = _dot(g2_ref[...], jnp.concatenate([y, y], axis=0)) + dwb_ref[...]

    for rs in subs:
        v = conv_ref[rs, :]
        mu = jnp.mean(v, axis=-1, keepdims=True)
        vc = v - mu
        var = jnp.mean(vc * vc, axis=-1, keepdims=True)
        vn = vc * lax.rsqrt(var + EPS) * lng_ref[...] + lnb_ref[...]
        act = vn * _sigmoid(vn)
        o_ref[rs, :] = _dot(act.astype(BF16), wo_ref[...]).astype(BF16)


def _conv_branch(h, wa, wb, ba, bb, dww, dwb, lng, lnb, wo, *, tm, seg):
    t = h.shape[0]
    assert 2 * seg >= seg + CONV_K - 1
    f2, g2, taps = _dft_consts(seg)
    dww_pad = jnp.zeros((CONV_K_PAD, CONV_DIM), F32).at[:CONV_K].set(dww)
    row = pl.BlockSpec((tm, D_MODEL), lambda i: (i, 0))
    vec = _const_spec((1, CONV_DIM))
    sq = _const_spec((D_MODEL, CONV_DIM))
    return pl.pallas_call(
        functools.partial(_conv_kernel, seg=seg),
        grid=(t // tm,),
        in_specs=[row, sq, sq, vec, vec, _const_spec((CONV_K_PAD, CONV_DIM)), vec, vec, vec, sq,
                  _const_spec(f2.shape), _const_spec(g2.shape), _const_spec(taps.shape)],
        out_specs=row,
        out_shape=jax.ShapeDtypeStruct((t, D_MODEL), BF16),
        scratch_shapes=[pltpu.VMEM((3 * seg, CONV_DIM), F32),
                        pltpu.VMEM((tm, CONV_DIM), F32)],
        compiler_params=_params(), name="conv_branch",
    )(h, wa, wb, ba.reshape(1, -1), bb.reshape(1, -1), dww_pad, dwb.reshape(1, -1), lng.reshape(1, -1),
      lnb.reshape(1, -1), wo, f2, g2, taps)


FEAT_ROWS = 64
FEAT_COLS = 512
FEAT_STRIDE = 9
SUBLANES = 8
FEAT_SPAN = SUBLANES * FEAT_STRIDE


def _aligned(v, m):
    return v if isinstance(v, int) else pl.multiple_of(v, m)


def _feat_kernel(hp_ref, h_ref, hn_ref, wx_ref, bx_ref, wdt_ref, bdt_ref, cw_ref, cb_ref,
                 xbc_ref, dt_ref, hfull_ref, raw0_ref, raw1_ref, act0_ref, act1_ref, *, lseq):
    tm = h_ref.shape[0]
    halo = BF16_ROWS
    i = pl.program_id(0)
    hfull_ref[0:halo, :] = hp_ref[...]
    hfull_ref[halo:halo + tm, :] = h_ref[...]
    hfull_ref[halo + tm:, :] = hn_ref[...]
    t0 = i * tm
    seq_lo = (t0 // lseq) * lseq
    halo_iota = lax.broadcasted_iota(jnp.int32, (halo, 1), 0)
    keep_lo = (t0 - halo + halo_iota) >= seq_lo
    keep_hi = (t0 + tm + halo_iota) < seq_lo + lseq
    first = halo - SSD_CONV_K // 2
    n_lane_blk = FEAT_COLS // LANES
    n_span = tm // FEAT_SPAN

    def project(j, raw_ref):
        cols = pl.ds(_aligned(j * FEAT_COLS, FEAT_COLS), FEAT_COLS)
        raw = _dot(hfull_ref[...], wx_ref[:, cols]) + bx_ref[:, cols]
        for lb in range(n_lane_blk):
            part = raw[:, lb * LANES:(lb + 1) * LANES]
            raw_ref[lb, 0:halo, :] = jnp.where(keep_lo, part[0:halo, :], 0.0)
            raw_ref[lb, halo:halo + tm, :] = part[halo:halo + tm, :]
            raw_ref[lb, halo + tm:, :] = jnp.where(keep_hi, part[halo + tm:, :], 0.0)

    def conv(j, raw_ref, act_ref):
        for lb in range(n_lane_blk):
            dst = pl.ds(_aligned(j * FEAT_COLS + lb * LANES, LANES), LANES)
            taps = [cw_ref[k:k + 1, dst] for k in range(SSD_CONV_K)]
            bias = cb_ref[:, dst]
            for sp in range(n_span):
                base = sp * FEAT_SPAN
                srcs = [raw_ref[lb, pl.ds(base + first + o, SUBLANES, stride=FEAT_STRIDE), :]
                        for o in range(FEAT_STRIDE + SSD_CONV_K - 1)]
                for r in range(FEAT_STRIDE):
                    acc = bias
                    for k in range(SSD_CONV_K):
                        acc = acc + srcs[r + k] * taps[k]
                    act_ref[lb, pl.ds(base + r, SUBLANES, stride=FEAT_STRIDE), :] = acc * _sigmoid(acc)
            for r0 in range(n_span * FEAT_SPAN, tm, FEAT_ROWS):
                nr = min(FEAT_ROWS, tm - r0)
                acc = bias
                for k in range(SSD_CONV_K):
                    acc = acc + raw_ref[lb, r0 + first + k:r0 + first + k + nr, :] * taps[k]
                act_ref[lb, r0:r0 + nr, :] = acc * _sigmoid(acc)
            xbc_ref[:, dst] = act_ref[lb].astype(BF16)

    n_blk = XBC_DIM // FEAT_COLS
    raws, acts = (raw0_ref, raw1_ref), (act0_ref, act1_ref)
    project(0, raws[0])
    for j in range(n_blk):
        if j + 1 < n_blk:
            project(j + 1, raws[(j + 1) % 2])
        conv(j, raws[j % 2], acts[j % 2])

    dtr = _dot(h_ref[...], wdt_ref[...]) + bdt_ref[...]
    sp = jnp.maximum(dtr, 0.0) + jnp.log1p(jnp.exp(-jnp.abs(dtr)))
    lane = lax.broadcasted_iota(jnp.int32, (1, LANES), 1)
    dt_ref[...] = jnp.where(lane < N_DIR * N_HEADS, sp, 0.0)


def _ssd_features(h, wx, bx, wdt, bdt, cw, cb, *, tm, lseq):
    t = h.shape[0]
    nb = tm // BF16_ROWS
    last = t // BF16_ROWS - 1
    return pl.pallas_call(
        functools.partial(_feat_kernel, lseq=lseq),
        grid=(t // tm,),
        in_specs=[pl.BlockSpec((BF16_ROWS, D_MODEL), lambda i: (jnp.maximum(i * nb - 1, 0), 0)),
                  pl.BlockSpec((tm, D_MODEL), lambda i: (i, 0)),
                  pl.BlockSpec((BF16_ROWS, D_MODEL), lambda i: (jnp.minimum((i + 1) * nb, last), 0)),
                  _const_spec((D_MODEL, XBC_DIM)), _const_spec((1, XBC_DIM)),
                  _const_spec((D_MODEL, LANES)), _const_spec((1, LANES)),
                  _const_spec((SSD_CONV_K, XBC_DIM)), _const_spec((1, XBC_DIM))],
        out_specs=[pl.BlockSpec((tm, XBC_DIM), lambda i: (i, 0)),
                   pl.BlockSpec((tm, LANES), lambda i: (i, 0))],
        out_shape=[jax.ShapeDtypeStruct((t, XBC_DIM), BF16), jax.ShapeDtypeStruct((t, LANES), F32)],
        scratch_shapes=[pltpu.VMEM((tm + 2 * BF16_ROWS, D_MODEL), BF16),
                        pltpu.VMEM((FEAT_COLS // LANES, tm + 2 * BF16_ROWS, LANES), F32),
                        pltpu.VMEM((FEAT_COLS // LANES, tm + 2 * BF16_ROWS, LANES), F32),
                        pltpu.VMEM((FEAT_COLS // LANES, tm, LANES), F32),
                        pltpu.VMEM((FEAT_COLS // LANES, tm, LANES), F32)],
        compiler_params=_params(), name="ssd_features",
    )(h, h, h, wx, bx.reshape(1, -1), wdt, bdt.reshape(1, -1), cw, cb.reshape(1, -1))


def _scan_kernel(xs_ref, bm_ref, cm_ref, dt_ref, dtn_ref, a_ref, s0_ref, y_ref, sf_ref, state_ref, tf_ref, tb_ref,
                 *, reverse, d):
    tb = xs_ref.shape[0]
    nch = tb // CHUNK
    j = pl.program_id(1)

    @pl.when(j == 0)
    def _():
        state_ref[...] = s0_ref[0]

    row_i = lax.broadcasted_iota(jnp.int32, (CHUNK, CHUNK), 0)
    col_i = lax.broadcasted_iota(jnp.int32, (CHUNK, CHUNK), 1)
    causal = (col_i >= row_i) if reverse else (row_i >= col_i)
    tri = causal.astype(F32)
    edge = 0 if reverse else CHUNK - 1
    lo64 = lax.broadcasted_iota(jnp.int32, (1, LANES), 1) < HEAD_DIM

    def pair(v0, v1):
        return jnp.where(lo64, v0, v1)

    def chunk_rows(c):
        ci = (nch - 1 - c) if reverse else c
        return pl.ds(_aligned(ci * CHUNK, CHUNK), CHUNK)

    def decay_terms(dt):
        acum = jnp.dot(tri, dt * (a_ref[...] * LOG2E), precision=lax.Precision.HIGHEST,
                       preferred_element_type=F32)
        acum_t = acum.T
        dt_t = dt.T
        a_edge_t = jnp.broadcast_to(acum_t[:, edge:edge + 1], (LANES, CHUNK))
        w_t = (jnp.exp2(a_edge_t - acum_t) * dt_t).astype(BF16)
        return acum, acum_t, w_t, dt_t.astype(BF16)

    def chunk_work(c, terms):
        rows = chunk_rows(c)
        acum, acum_t, w_t, dt_tb = terms
        zeros_half = jnp.zeros((CHUNK, LANES), BF16)
        staged = []
        for g in range(N_GROUPS):
            bg = bm_ref[rows, g * D_STATE:(g + 1) * D_STATE]
            cg = cm_ref[rows, g * D_STATE:(g + 1) * D_STATE]
            bg_t = bg.astype(F32).T.astype(BF16)
            cb = jnp.where(causal, _dot(cg, bg_t), 0.0).astype(BF16)
            s_in = state_ref[g]
            y_off = _dot(cg, s_in.astype(BF16))
            m_parts, bw_parts, a_cols, rhs_parts = [], [], [], []
            for hh in range(HEADS_PER_GROUP):
                q = d * N_HEADS + g * HEADS_PER_GROUP + hh
                a_col = jnp.broadcast_to(acum[:, q:q + 1], (CHUNK, CHUNK))
                decay = jnp.exp2(jnp.minimum(a_col - acum_t[q:q + 1, :], 0.0))
                m_parts.append(cb * decay.astype(BF16) * dt_tb[q:q + 1, :])
                bw_parts.append(bg_t * w_t[q:q + 1, :])
                a_cols.append(a_col)
                half = xs_ref[rows, g * GROUP_W + (hh // 2) * LANES:g * GROUP_W + (hh // 2 + 1) * LANES]
                own = jnp.where(lo64 if hh % 2 == 0 else jnp.logical_not(lo64), half, zeros_half)
                rhs_parts.append(jnp.concatenate([own, zeros_half] if hh < 2 else [zeros_half, own], axis=1))
            lhs = jnp.concatenate([jnp.concatenate(m_parts, axis=1), jnp.concatenate(bw_parts, axis=1)], axis=0)
            e_grp = jnp.exp2(jnp.concatenate([pair(a_cols[0], a_cols[1]), pair(a_cols[2], a_cols[3])], axis=1))
            staged.append((lhs, jnp.concatenate(rhs_parts, axis=0), s_in, y_off, e_grp))
        for g, (lhs, rhs, s_in, y_off, e_grp) in enumerate(staged):
            res = _dot(lhs, rhs)
            y = res[:CHUNK, :] + y_off * e_grp
            y_ref[rows, g * GROUP_W:(g + 1) * GROUP_W] = y.astype(BF16)
            state_ref[g] = s_in * e_grp[edge:edge + 1, :] + res[CHUNK:, :]

    def save_terms(terms):
        acum, acum_t, w_t, dt_tb = terms
        tf_ref[0], tf_ref[1], tb_ref[0], tb_ref[1] = acum, acum_t, w_t, dt_tb

    @pl.when(j == 0)
    def _():
        save_terms(decay_terms(dt_ref[chunk_rows(0), :]))

    def loop_body(c, terms):
        next_terms = decay_terms(dt_ref[chunk_rows(c + 1), :])
        chunk_work(c, terms)
        return next_terms

    terms = lax.fori_loop(0, nch - 1, loop_body, (tf_ref[0], tf_ref[1], tb_ref[0], tb_ref[1]))
    next_terms = decay_terms(dtn_ref[chunk_rows(0), :])
    chunk_work(nch - 1, terms)
    save_terms(next_terms)

    @pl.when(j == pl.num_programs(1) - 1)
    def _():
        sf_ref[0] = state_ref[...]


def _ssd_scan(xbc, dt, a128, s0, *, bsz, lseq, tb, reverse, d):
    nblk = lseq // tb

    def blk(b, j):
        return b * nblk + ((nblk - 1 - j) if reverse else j)

    n_xs = D_INNER // (N_GROUPS * D_STATE)
    st_spec = pl.BlockSpec((1, N_GROUPS, D_STATE, GROUP_W), lambda b, j: (b, 0, 0, 0))
    return pl.pallas_call(
        functools.partial(_scan_kernel, reverse=reverse, d=d),
        grid=(bsz, nblk),
        in_specs=[pl.BlockSpec((tb, D_INNER), lambda b, j: (blk(b, j), 0)),
                  pl.BlockSpec((tb, N_GROUPS * D_STATE), lambda b, j: (blk(b, j), n_xs)),
                  pl.BlockSpec((tb, N_GROUPS * D_STATE), lambda b, j: (blk(b, j), n_xs + 1)),
                  pl.BlockSpec((tb, LANES), lambda b, j: (blk(b, j), 0)),
                  pl.BlockSpec((tb, LANES), lambda b, j: (blk(b, jnp.minimum(j + 1, nblk - 1)), 0)),
                  pl.BlockSpec((1, LANES), lambda b, j: (0, 0)),
                  st_spec],
        out_specs=[pl.BlockSpec((tb, D_INNER), lambda b, j: (blk(b, j), 0)), st_spec],
        out_shape=[jax.ShapeDtypeStruct((bsz * lseq, D_INNER), BF16),
                   jax.ShapeDtypeStruct((bsz, N_GROUPS, D_STATE, GROUP_W), F32)],
        scratch_shapes=[pltpu.VMEM((N_GROUPS, D_STATE, GROUP_W), F32),
                        pltpu.VMEM((2, LANES, CHUNK), F32), pltpu.VMEM((2, LANES, CHUNK), BF16)],
        compiler_params=_params(2), name="ssd_scan_bwd" if reverse else "ssd_scan_fwd",
    )(xbc, xbc, xbc, dt, dt, a128, s0)


MERGE_SUB = 256


def _merge_kernel(x_ref, h_ref, yf_ref, yb_ref, xs_ref, yc_ref, mod_ref, wz_ref, bz_ref, wg_ref, bg_ref,
                  dsk_ref, nrm_ref, wso_ref, wo_ref, o_ref, *, k_gate):
    tm = x_ref.shape[0]
    subs = [slice(r, r + MERGE_SUB) for r in range(0, tm, MERGE_SUB)]
    proj = []
    for rs in subs:
        hb = h_ref[rs, :]
        proj.append((_dot(hb, wz_ref[...]) + bz_ref[...], _dot(hb, wg_ref[...]) + bg_ref[...]))
    y_ssd = []
    for rs, (z, _) in zip(subs, proj):
        y = yf_ref[rs, :].astype(F32) + yb_ref[rs, :].astype(F32) + dsk_ref[...] * xs_ref[rs, :].astype(F32)
        y = y * (z * _sigmoid(z))
        parts = []
        for g in range(N_GROUPS):
            yg = y[:, g * GROUP_W:(g + 1) * GROUP_W]
            ms = jnp.mean(yg * yg, axis=-1, keepdims=True)
            parts.append(yg * lax.rsqrt(ms + EPS))
        yn = jnp.concatenate(parts, axis=1) * nrm_ref[...]
        y_ssd.append(_dot(yn.astype(BF16), wso_ref[...]))
    for rs, (_, graw), y_s in zip(subs, proj, y_ssd):
        m = _sigmoid(graw[:, :D_MODEL]) * yc_ref[rs, :].astype(F32) + _sigmoid(graw[:, D_MODEL:]) * y_s
        mix = _dot(m.astype(BF16), wo_ref[...])
        o_ref[rs, :] = x_ref[rs, :] + mod_ref[0, k_gate:k_gate + 1, :] * mix


def _merge(x, h, yf, yb, xbc, yc, mods, mod_map, wz, bz, wg, bg, dsk, nrm, wso, wo, *, tm, k_gate):
    t = x.shape[0]
    row = pl.BlockSpec((tm, D_MODEL), lambda i: (i, 0))
    wide = pl.BlockSpec((tm, D_INNER), lambda i: (i, 0))
    return pl.pallas_call(
        functools.partial(_merge_kernel, k_gate=k_gate),
        grid=(t // tm,),
        in_specs=[row, row, wide, wide, wide, row,
                  pl.BlockSpec((1, N_MOD, D_MODEL), mod_map),
                  _const_spec((D_MODEL, D_INNER)), _const_spec((1, D_INNER)),
                  _const_spec((D_MODEL, 2 * D_MODEL)), _const_spec((1, 2 * D_MODEL)),
                  _const_spec((1, D_INNER)), _const_spec((1, D_INNER)),
                  _const_spec((D_INNER, D_MODEL)), _const_spec((D_MODEL, D_MODEL))],
        out_specs=row,
        out_shape=jax.ShapeDtypeStruct((t, D_MODEL), F32),
        compiler_params=_params(), name="mixer_merge",
    )(x, h, yf, yb, xbc, yc, mods, wz, bz.reshape(1, -1), wg, bg.reshape(1, -1), dsk.reshape(1, -1),
      nrm.reshape(1, -1), wso, wo)


def kernel(x, c, ctx, c_ctx, w_ada, b_ada, ffn1_norm, ffn1_w_gate, ffn1_w_up, ffn1_w_down, mix_norm, w_in, b_in, conv_dw_w, conv_dw_b, conv_ln_g, conv_ln_b, conv_w_out, ssd_conv_w, ssd_conv_b, ssd_dt_bias, ssd_a_log, ssd_d, ssd_norm, ssd_w_out, w_out, ffn2_norm, ffn2_w_gate, ffn2_w_up, ffn2_w_down, final_norm):
    bsz, seq, _ = x.shape
    ctx_len = ctx.shape[1]
    depth = w_ada.shape[0]
    assert seq % 1024 == 0 and ctx_len % 256 == 0 and bsz + 1 <= 8

    tm_x, tm_c = 512, 256
    xf = x.reshape(bsz * seq, D_MODEL)
    cf = ctx.reshape(bsz * ctx_len, D_MODEL)
    tm_ffn = 2 * tm_x
    tm_ca = bsz * ctx_len
    assert tm_ca % MERGE_SUB == 0 and tm_ca <= tm_ffn
    x_map = lambda i: ((i * tm_x) // seq, 0, 0)
    x_map_ffn = lambda i: ((i * tm_ffn) // seq, 0, 0)
    c_map = lambda i: (bsz, 0, 0)

    cvec = jnp.zeros((8, D_MODEL), F32).at[:bsz].set(c).at[bsz].set(c_ctx)
    mods_all = _mods(cvec, w_ada, b_ada)[:, :bsz + 1].reshape(depth, bsz + 1, N_MOD, D_MODEL)

    s1 = 2 * CONV_DIM
    s2 = s1 + D_INNER
    s3 = s2 + XBC_DIM
    s4 = s3 + N_DIR * N_HEADS
    zero_state = jnp.zeros((bsz, N_GROUPS, D_STATE, GROUP_W), F32)

    for i in range(depth):
        last = i == depth - 1
        mods = mods_all[i]
        bf = lambda w: w.astype(BF16)
        w1 = (bf(ffn1_w_gate[i]), bf(ffn1_w_up[i]), bf(ffn1_w_down[i]))
        w2 = (bf(ffn2_w_gate[i]), bf(ffn2_w_up[i]), bf(ffn2_w_down[i]))
        wi, bi = w_in[i], b_in[i]
        wa, wb = bf(wi[:, :CONV_DIM]), bf(wi[:, CONV_DIM:s1])
        ba, bb = bi[:CONV_DIM], bi[CONV_DIM:s1]
        wz, bz = bf(wi[:, s1:s2]), bi[s1:s2]
        wx, bx = bf(wi[:, s2:s3]), bi[s2:s3]
        n_dt = N_DIR * N_HEADS
        wdt = bf(jnp.zeros((D_MODEL, LANES), F32).at[:, :n_dt].set(wi[:, s3:s4]))
        pad_dt = lambda v: jnp.zeros((LANES,), F32).at[:n_dt].set(v.reshape(-1))
        bdt = pad_dt(bi[s3:s4] + ssd_dt_bias[i].reshape(-1))
        a128 = pad_dt(-jnp.exp(ssd_a_log[i].astype(F32))).reshape(1, LANES)
        wgt, bgt = bf(wi[:, s4:]), bi[s4:]
        dsk = jnp.repeat(ssd_d[i], HEAD_DIM)
        conv_w = (wa, wb, ba, bb, conv_dw_w[i], conv_dw_b[i], conv_ln_g[i], conv_ln_b[i], bf(conv_w_out[i]))
        merge_w = (wz, bz, wgt, bgt, dsk, ssd_norm[i], bf(ssd_w_out[i]), bf(w_out[i]))
        feat_w = (wx, bx, wdt, bdt, ssd_conv_w[i], ssd_conv_b[i])

        xf, hx = _ffn(xf, mods, x_map_ffn, ffn1_norm[i], *w1, tm=tm_ffn, k0=0, post="mod", g2=mix_norm[i], k1=3)
        cf, hc = _ffn(cf, mods, c_map, ffn1_norm[i], *w1, tm=tm_ca, k0=0, post="mod", g2=mix_norm[i], k1=3)

        xbc_c, dt_c = _ssd_features(hc, *feat_w, tm=tm_c, lseq=ctx_len)
        xbc_x, dt_x = _ssd_features(hx, *feat_w, tm=tm_x, lseq=seq)
        scan_c = functools.partial(_ssd_scan, xbc_c, dt_c, a128, bsz=bsz, lseq=ctx_len, tb=256)
        scan_x = functools.partial(_ssd_scan, xbc_x, dt_x, a128, bsz=bsz, lseq=seq, tb=1024)
        yf_c, s_f = scan_c(zero_state, reverse=False, d=0)
        yb_c, s_b = scan_c(zero_state, reverse=True, d=1)
        yf_x, _ = scan_x(s_f, reverse=False, d=0)
        yb_x, _ = scan_x(s_b, reverse=True, d=1)

        yc_x = _conv_branch(hx, *conv_w, tm=tm_x, seg=GRID_W)
        xf = _merge(xf, hx, yf_x, yb_x, xbc_x, yc_x, mods, x_map, *merge_w, tm=tm_x, k_gate=5)
        if not last:
            yc_c = _conv_branch(hc, *conv_w, tm=tm_ca, seg=ctx_len)
            cf = _merge(cf, hc, yf_c, yb_c, xbc_c, yc_c, mods, c_map, *merge_w, tm=tm_ca, k_gate=5)

        if last:
            xf = _ffn(xf, mods, x_map_ffn, ffn2_norm[i], *w2, tm=tm_ffn, k0=6, post="final", g2=final_norm)
        else:
            xf = _ffn(xf, mods, x_map_ffn, ffn2_norm[i], *w2, tm=tm_ffn, k0=6)
            cf = _ffn(cf, mods, c_map, ffn2_norm[i], *w2, tm=tm_ca, k0=6)
    return xf.reshape(bsz, seq, D_MODEL)
```

```python
import functools

import jax
import jax.numpy as jnp
import numpy as np
from jax import lax
from jax.experimental import pallas as pl
from jax.experimental.pallas import tpu as pltpu

F32 = jnp.float32
BF16 = jnp.bfloat16

D_MODEL = 1024
N_MOD = 9
D_FF = 2816
CONV_DIM = D_MODEL
CONV_K = 31
D_INNER = 2 * D_MODEL
HEAD_DIM = 64
N_HEADS = D_INNER // HEAD_DIM
N_GROUPS = 8
HEADS_PER_GROUP = N_HEADS // N_GROUPS
GROUP_W = HEADS_PER_GROUP * HEAD_DIM
D_STATE = 128
SSD_CONV_K = 5
CHUNK = 128
N_DIR = 2
XBC_DIM = D_INNER + 2 * N_GROUPS * D_STATE
GRID_W = 64
EPS = 1e-6

LANES = 128
BF16_ROWS = 16
FF_CHUNK = 256
VMEM_LIMIT = 58 * 1024 * 1024


def _dot(a, b):
    return jnp.dot(a, b, preferred_element_type=F32)


LOG2E = 1.4426950408889634


def _sigmoid(v):
    return 1.0 / (1.0 + jnp.exp2(v * (-LOG2E)))


def _const_spec(shape):
    nd = len(shape)
    return pl.BlockSpec(shape, lambda *_: (0,) * nd, pipeline_mode=pl.Buffered(1))


def _params(n_axes=1, flags=None):
    return pltpu.CompilerParams(dimension_semantics=("arbitrary",) * n_axes,
                                vmem_limit_bytes=VMEM_LIMIT, flags=flags)


def _mods_kernel(c_ref, w_ref, b_ref, o_ref):
    c = c_ref[...]
    act = c * _sigmoid(c)
    o_ref[0] = jnp.dot(act, w_ref[0], precision=lax.Precision.HIGHEST,
                       preferred_element_type=F32) + b_ref[0]


def _mods(cvec, w_ada, b_ada):
    depth, _, n_out = w_ada.shape
    tn = D_MODEL
    return pl.pallas_call(
        _mods_kernel,
        grid=(depth, n_out // tn),
        in_specs=[pl.BlockSpec((8, D_MODEL), lambda l, j: (0, 0)),
                  pl.BlockSpec((1, D_MODEL, tn), lambda l, j: (l, 0, j)),
                  pl.BlockSpec((1, 1, tn), lambda l, j: (l, 0, j))],
        out_specs=pl.BlockSpec((1, 8, tn), lambda l, j: (l, 0, j)),
        out_shape=jax.ShapeDtypeStruct((depth, 8, n_out), F32),
        compiler_params=_params(2),
        name="ada_mods",
    )(cvec, w_ada, b_ada.reshape(depth, 1, n_out))


def _rms(v, g):
    ms = jnp.mean(v * v, axis=-1, keepdims=True)
    return v * lax.rsqrt(ms + EPS) * g


FFN_SUB = 512
FFN_LAG = 2


def _ffn_kernel(*refs, k0, post, k1):
    if post == "mod":
        x_ref, mod_ref, g_ref, wg_ref, wu_ref, wd_ref, g2_ref, o_ref, h_ref = refs
    elif post == "final":
        x_ref, mod_ref, g_ref, wg_ref, wu_ref, wd_ref, g2_ref, o_ref = refs
    else:
        x_ref, mod_ref, g_ref, wg_ref, wu_ref, wd_ref, o_ref = refs
    tm = x_ref.shape[0]
    shift = mod_ref[0, k0:k0 + 1, :]
    scale = mod_ref[0, k0 + 1:k0 + 2, :]
    gate = mod_ref[0, k0 + 2:k0 + 3, :]
    n_chunks = D_FF // FF_CHUNK
    subs = [slice(r, r + min(tm, FFN_SUB)) for r in range(0, tm, FFN_SUB)]

    def gate_up(hb, c):
        sl = slice(c * FF_CHUNK, (c + 1) * FF_CHUNK)
        return _dot(hb, wg_ref[:, sl]), _dot(hb, wu_ref[:, sl])

    def epilogue(rs, acc):
        y = x_ref[rs, :] + 0.5 * gate * acc
        if post == "final":
            o_ref[rs, :] = _rms(y, g2_ref[...])
            return
        o_ref[rs, :] = y
        if post == "mod":
            shift2 = mod_ref[0, k1:k1 + 1, :]
            scale2 = mod_ref[0, k1 + 1:k1 + 2, :]
            h_ref[rs, :] = (_rms(y, g2_ref[...]) * (1.0 + scale2) + shift2).astype(BF16)

    hbs, accs, pending = {}, {}, {}
    for t in range(n_chunks + FFN_LAG * (len(subs) - 1) + 1):
        for s, rs in enumerate(subs):
            c = t - FFN_LAG * s
            if c == 0:
                hbs[s] = (_rms(x_ref[rs, :], g_ref[...]) * (1.0 + scale) + shift).astype(BF16)
                accs[s] = jnp.zeros(hbs[s].shape, F32)
                pending[s] = gate_up(hbs[s], 0)
            if 0 <= c < n_chunks:
                g, u = pending[s]
                if c + 1 < n_chunks:
                    pending[s] = gate_up(hbs[s], c + 1)
                a = (g * _sigmoid(g)) * u
                accs[s] = accs[s] + _dot(a.astype(BF16), wd_ref[c * FF_CHUNK:(c + 1) * FF_CHUNK, :])
            if c == n_chunks:
                epilogue(rs, accs[s])


def _ffn(x, mods, mod_map, norm_g, wg, wu, wd, *, tm, k0, post="none", g2=None, k1=0):
    t = x.shape[0]
    row = pl.BlockSpec((tm, D_MODEL), lambda i: (i, 0))
    in_specs = [row,
                pl.BlockSpec((1, N_MOD, D_MODEL), mod_map),
                _const_spec((1, D_MODEL)),
                _const_spec((D_MODEL, D_FF)), _const_spec((D_MODEL, D_FF)),
                _const_spec((D_FF, D_MODEL))]
    args = [x, mods, norm_g.reshape(1, D_MODEL), wg, wu, wd]
    if post != "none":
        in_specs.append(_const_spec((1, D_MODEL)))
        args.append(g2.reshape(1, D_MODEL))
    if post == "mod":
        out_specs = [row, row]
        out_shape = [jax.ShapeDtypeStruct((t, D_MODEL), F32), jax.ShapeDtypeStruct((t, D_MODEL), BF16)]
    else:
        out_specs = row
        out_shape = jax.ShapeDtypeStruct((t, D_MODEL), F32)
    return pl.pallas_call(
        functools.partial(_ffn_kernel, k0=k0, post=post, k1=k1),
        grid=(t // tm,), in_specs=in_specs, out_specs=out_specs, out_shape=out_shape,
        compiler_params=_params(), name="half_ffn_" + post,
    )(*args)


CONV_K_PAD = 32
CONV_SUB = 256


def _split_bf16(m):
    hi = jnp.asarray(m, F32).astype(BF16)
    lo = (jnp.asarray(m, F32) - hi.astype(F32)).astype(BF16)
    return hi, lo


def _dft_consts(seg):
    n = 2 * seg
    t = np.arange(seg)
    r = np.arange(seg)
    th = 2.0 * np.pi * np.outer(r, t) / n
    fwd = np.concatenate([np.cos(th), np.sin(th)], axis=0)
    fwd[seg] = np.cos(np.pi * t)
    inv = np.concatenate([2.0 * np.cos(th.T), 2.0 * np.sin(th.T)], axis=1) / n
    inv[:, 0] = 1.0 / n
    inv[:, seg] = np.cos(np.pi * t) / n
    shift = CONV_K // 2 - np.arange(CONV_K)
    ph = 2.0 * np.pi * np.outer(r, shift) / n
    re, im = np.cos(ph), -np.sin(ph)
    re_nyq = re.copy()
    re_nyq[0] = np.cos(np.pi * shift)
    taps = np.zeros((3 * seg, CONV_K_PAD))
    taps[:, :CONV_K] = np.concatenate([re, im, re_nyq], axis=0)
    f2 = jnp.concatenate(_split_bf16(fwd), axis=1)
    g2 = jnp.concatenate(_split_bf16(inv), axis=1)
    return f2, g2, jnp.asarray(taps, F32)


def _conv_kernel(h_ref, wa_ref, wb_ref, ba_ref, bb_ref, dww_ref, dwb_ref, lng_ref, lnb_ref, wo_ref,
                 f2_ref, g2_ref, taps_ref, o_ref, coef_ref, conv_ref, *, seg):
    tm = h_ref.shape[0]

    @pl.when(pl.program_id(0) == 0)
    def _():
        coef_ref[...] = jnp.dot(taps_ref[...], dww_ref[...], precision=lax.Precision.HIGHEST,
                                preferred_element_type=F32)

    sub = max(seg, min(tm, CONV_SUB))
    subs = [slice(r, r + sub) for r in range(0, tm, sub)]
    glu = []
    for rs in subs:
        hb = h_ref[rs, :]
        glu.append((_dot(hb, wa_ref[...]) + ba_ref[...], _dot(hb, wb_ref[...]) + bb_ref[...]))
    specs = []
    for a, b in glu:
        ub = (a * _sigmoid(b)).astype(BF16)
        for s in range(sub // seg):
            us = ub[s * seg:(s + 1) * seg, :]
            specs.append(_dot(f2_ref[...], jnp.concatenate([us, us], axis=0)))
    h_re = coef_ref[0:seg, :]
    h_im = coef_ref[seg:2 * seg, :]
    h_re_nyq = coef_ref[2 * seg:3 * seg, :]
    for s, spec in enumerate(specs):
        co, si = spec[:seg, :], spec[seg:, :]
        y = jnp.concatenate([co * h_re + si * h_im, si * h_re_nyq - co * h_im], axis=0).astype(BF16)
        conv_ref[s * seg:(s + 1) * seg, :] = _dot(g2_ref[...], jnp.concatenate([y, y], axis=0)) + dwb_ref[...]

    for rs in subs:
        v = conv_ref[rs, :]
        mu = jnp.mean(v, axis=-1, keepdims=True)
        vc = v - mu
        var = jnp.mean(vc * vc, axis=-1, keepdims=True)
        vn = vc * lax.rsqrt(var + EPS) * lng_ref[...] + lnb_ref[...]
        act = vn * _sigmoid(vn)
        o_ref[rs, :] = _dot(act.astype(BF16), wo_ref[...]).astype(BF16)


def _conv_branch(h, wa, wb, ba, bb, dww, dwb, lng, lnb, wo, *, tm, seg):
    t = h.shape[0]
    assert 2 * seg >= seg + CONV_K - 1
    f2, g2, taps = _dft_consts(seg)
    dww_pad = jnp.zeros((CONV_K_PAD, CONV_DIM), F32).at[:CONV_K].set(dww)
    row = pl.BlockSpec((tm, D_MODEL), lambda i: (i, 0))
    vec = _const_spec((1, CONV_DIM))
    sq = _const_spec((D_MODEL, CONV_DIM))
    return pl.pallas_call(
        functools.partial(_conv_kernel, seg=seg),
        grid=(t // tm,),
        in_specs=[row, sq, sq, vec, vec, _const_spec((CONV_K_PAD, CONV_DIM)), vec, vec, vec, sq,
                  _const_spec(f2.shape), _const_spec(g2.shape), _const_spec(taps.shape)],
        out_specs=row,
        out_shape=jax.ShapeDtypeStruct((t, D_MODEL), BF16),
        scratch_shapes=[pltpu.VMEM((3 * seg, CONV_DIM), F32),
                        pltpu.VMEM((tm, CONV_DIM), F32)],
        compiler_params=_params(), name="conv_branch",
    )(h, wa, wb, ba.reshape(1, -1), bb.reshape(1, -1), dww_pad, dwb.reshape(1, -1), lng.reshape(1, -1),
      lnb.reshape(1, -1), wo, f2, g2, taps)


FEAT_ROWS = 64
FEAT_COLS = 512
FEAT_STRIDE = 9
SUBLANES = 8
FEAT_SPAN = SUBLANES * FEAT_STRIDE


def _aligned(v, m):
    return v if isinstance(v, int) else pl.multiple_of(v, m)


def _feat_kernel(hp_ref, h_ref, hn_ref, wx_ref, bx_ref, wdt_ref, bdt_ref, cw_ref, cb_ref,
                 xbc_ref, dt_ref, hfull_ref, raw0_ref, raw1_ref, act0_ref, act1_ref, *, lseq):
    tm = h_ref.shape[0]
    halo = BF16_ROWS
    i = pl.program_id(0)
    hfull_ref[0:halo, :] = hp_ref[...]
    hfull_ref[halo:halo + tm, :] = h_ref[...]
    hfull_ref[halo + tm:, :] = hn_ref[...]
    t0 = i * tm
    seq_lo = (t0 // lseq) * lseq
    halo_iota = lax.broadcasted_iota(jnp.int32, (halo, 1), 0)
    keep_lo = (t0 - halo + halo_iota) >= seq_lo
    keep_hi = (t0 + tm + halo_iota) < seq_lo + lseq
    first = halo - SSD_CONV_K // 2
    n_lane_blk = FEAT_COLS // LANES
    n_span = tm // FEAT_SPAN

    def project(j, raw_ref):
        cols = pl.ds(_aligned(j * FEAT_COLS, FEAT_COLS), FEAT_COLS)
        raw = _dot(hfull_ref[...], wx_ref[:, cols]) + bx_ref[:, cols]
        for lb in range(n_lane_blk):
            part = raw[:, lb * LANES:(lb + 1) * LANES]
            raw_ref[lb, 0:halo, :] = jnp.where(keep_lo, part[0:halo, :], 0.0)
            raw_ref[lb, halo:halo + tm, :] = part[halo:halo + tm, :]
            raw_ref[lb, halo + tm:, :] = jnp.where(keep_hi, part[halo + tm:, :], 0.0)

    def conv(j, raw_ref, act_ref):
        for lb in range(n_lane_blk):
            dst = pl.ds(_aligned(j * FEAT_COLS + lb * LANES, LANES), LANES)
            taps = [cw_ref[k:k + 1, dst] for k in range(SSD_CONV_K)]
            bias = cb_ref[:, dst]
            for sp in range(n_span):
                base = sp * FEAT_SPAN
                srcs = [raw_ref[lb, pl.ds(base + first + o, SUBLANES, stride=FEAT_STRIDE), :]
                        for o in range(FEAT_STRIDE + SSD_CONV_K - 1)]
                for r in range(FEAT_STRIDE):
                    acc = bias
                    for k in range(SSD_CONV_K):
                        acc = acc + srcs[r + k] * taps[k]
                    act_ref[lb, pl.ds(base + r, SUBLANES, stride=FEAT_STRIDE), :] = acc * _sigmoid(acc)
            for r0 in range(n_span * FEAT_SPAN, tm, FEAT_ROWS):
                nr = min(FEAT_ROWS, tm - r0)
                acc = bias
                for k in range(SSD_CONV_K):
                    acc = acc + raw_ref[lb, r0 + first + k:r0 + first + k + nr, :] * taps[k]
                act_ref[lb, r0:r0 + nr, :] = acc * _sigmoid(acc)
            xbc_ref[:, dst] = act_ref[lb].astype(BF16)

    n_blk = XBC_DIM // FEAT_COLS
    raws, acts = (raw0_ref, raw1_ref), (act0_ref, act1_ref)
    project(0, raws[0])
    for j in range(n_blk):
        if j + 1 < n_blk:
            project(j + 1, raws[(j + 1) % 2])
        conv(j, raws[j % 2], acts[j % 2])

    dtr = _dot(h_ref[...], wdt_ref[...]) + bdt_ref[...]
    sp = jnp.maximum(dtr, 0.0) + jnp.log1p(jnp.exp(-jnp.abs(dtr)))
    lane = lax.broadcasted_iota(jnp.int32, (1, LANES), 1)
    dt_ref[...] = jnp.where(lane < N_DIR * N_HEADS, sp, 0.0)


def _ssd_features(h, wx, bx, wdt, bdt, cw, cb, *, tm, lseq):
    t = h.shape[0]
    nb = tm // BF16_ROWS
    last = t // BF16_ROWS - 1
    return pl.pallas_call(
        functools.partial(_feat_kernel, lseq=lseq),
        grid=(t // tm,),
        in_specs=[pl.BlockSpec((BF16_ROWS, D_MODEL), lambda i: (jnp.maximum(i * nb - 1, 0), 0)),
                  pl.BlockSpec((tm, D_MODEL), lambda i: (i, 0)),
                  pl.BlockSpec((BF16_ROWS, D_MODEL), lambda i: (jnp.minimum((i + 1) * nb, last), 0)),
                  _const_spec((D_MODEL, XBC_DIM)), _const_spec((1, XBC_DIM)),
                  _const_spec((D_MODEL, LANES)), _const_spec((1, LANES)),
                  _const_spec((SSD_CONV_K, XBC_DIM)), _const_spec((1, XBC_DIM))],
        out_specs=[pl.BlockSpec((tm, XBC_DIM), lambda i: (i, 0)),
                   pl.BlockSpec((tm, LANES), lambda i: (i, 0))],
        out_shape=[jax.ShapeDtypeStruct((t, XBC_DIM), BF16), jax.ShapeDtypeStruct((t, LANES), F32)],
        scratch_shapes=[pltpu.VMEM((tm + 2 * BF16_ROWS, D_MODEL), BF16),
                        pltpu.VMEM((FEAT_COLS // LANES, tm + 2 * BF16_ROWS, LANES), F32),
                        pltpu.VMEM((FEAT_COLS // LANES, tm + 2 * BF16_ROWS, LANES), F32),
                        pltpu.VMEM((FEAT_COLS // LANES, tm, LANES), F32),
                        pltpu.VMEM((FEAT_COLS // LANES, tm, LANES), F32)],
        compiler_params=_params(), name="ssd_features",
    )(h, h, h, wx, bx.reshape(1, -1), wdt, bdt.reshape(1, -1), cw, cb.reshape(1, -1))


def _scan_kernel(xs_ref, bm_ref, cm_ref, dt_ref, dtn_ref, a_ref, s0_ref, y_ref, sf_ref, state_ref, tf_ref, tb_ref,
                 *, reverse, d):
    tb = xs_ref.shape[0]
    nch = tb // CHUNK
    j = pl.program_id(1)

    @pl.when(j == 0)
    def _():
        state_ref[...] = s0_ref[0]

    row_i = lax.broadcasted_iota(jnp.int32, (CHUNK, CHUNK), 0)
    col_i = lax.broadcasted_iota(jnp.int32, (CHUNK, CHUNK), 1)
    causal = (col_i >= row_i) if reverse else (row_i >= col_i)
    tri = causal.astype(F32)
    edge = 0 if reverse else CHUNK - 1
    lo64 = lax.broadcasted_iota(jnp.int32, (1, LANES), 1) < HEAD_DIM

    def pair(v0, v1):
        return jnp.where(lo64, v0, v1)

    def chunk_rows(c):
        ci = (nch - 1 - c) if reverse else c
        return pl.ds(_aligned(ci * CHUNK, CHUNK), CHUNK)

    def decay_terms(dt):
        acum = jnp.dot(tri, dt * (a_ref[...] * LOG2E), precision=lax.Precision.HIGHEST,
                       preferred_element_type=F32)
        acum_t = acum.T
        dt_t = dt.T
        a_edge_t = jnp.broadcast_to(acum_t[:, edge:edge + 1], (LANES, CHUNK))
        w_t = (jnp.exp2(a_edge_t - acum_t) * dt_t).astype(BF16)
        return acum, acum_t, w_t, dt_t.astype(BF16)

    def chunk_work(c, terms):
        rows = chunk_rows(c)
        acum, acum_t, w_t, dt_tb = terms
        zeros_half = jnp.zeros((CHUNK, LANES), BF16)
        staged = []
        for g in range(N_GROUPS):
            bg = bm_ref[rows, g * D_STATE:(g + 1) * D_STATE]
            cg = cm_ref[rows, g * D_STATE:(g + 1) * D_STATE]
            bg_t = bg.astype(F32).T.astype(BF16)
            cb = jnp.where(causal, _dot(cg, bg_t), 0.0).astype(BF16)
            s_in = state_ref[g]
            y_off = _dot(cg, s_in.astype(BF16))
            m_parts, bw_parts, a_cols, rhs_parts = [], [], [], []
            for hh in range(HEADS_PER_GROUP):
                q = d * N_HEADS + g * HEADS_PER_GROUP + hh
                a_col = jnp.broadcast_to(acum[:, q:q + 1], (CHUNK, CHUNK))
                decay = jnp.exp2(jnp.minimum(a_col - acum_t[q:q + 1, :], 0.0))
                m_parts.append(cb * decay.astype(BF16) * dt_tb[q:q + 1, :])
                bw_parts.append(bg_t * w_t[q:q + 1, :])
                a_cols.append(a_col)
                half = xs_ref[rows, g * GROUP_W + (hh // 2) * LANES:g * GROUP_W + (hh // 2 + 1) * LANES]
                own = jnp.where(lo64 if hh % 2 == 0 else jnp.logical_not(lo64), half, zeros_half)
                rhs_parts.append(jnp.concatenate([own, zeros_half] if hh < 2 else [zeros_half, own], axis=1))
            lhs = jnp.concatenate([jnp.concatenate(m_parts, axis=1), jnp.concatenate(bw_parts, axis=1)], axis=0)
            e_grp = jnp.exp2(jnp.concatenate([pair(a_cols[0], a_cols[1]), pair(a_cols[2], a_cols[3])], axis=1))
            staged.append((lhs, jnp.concatenate(rhs_parts, axis=0), s_in, y_off, e_grp))
        for g, (lhs, rhs, s_in, y_off, e_grp) in enumerate(staged):
            res = _dot(lhs, rhs)
            y = res[:CHUNK, :] + y_off * e_grp
            y_ref[rows, g * GROUP_W:(g + 1) * GROUP_W] = y.astype(BF16)
            state_ref[g] = s_in * e_grp[edge:edge + 1, :] + res[CHUNK:, :]

    def save_terms(terms):
        acum, acum_t, w_t, dt_tb = terms
        tf_ref[0], tf_ref[1], tb_ref[0], tb_ref[1] = acum, acum_t, w_t, dt_tb

    @pl.when(j == 0)
    def _():
        save_terms(decay_terms(dt_ref[chunk_rows(0), :]))

    def loop_body(c, terms):
        next_terms = decay_terms(dt_ref[chunk_rows(c + 1), :])
        chunk_work(c, terms)
        return next_terms

    terms = lax.fori_loop(0, nch - 1, loop_body, (tf_ref[0], tf_ref[1], tb_ref[0], tb_ref[1]))
    next_terms = decay_terms(dtn_ref[chunk_rows(0), :])
    chunk_work(nch - 1, terms)
    save_terms(next_terms)

    @pl.when(j == pl.num_programs(1) - 1)
    def _():
        sf_ref[0] = state_ref[...]


def _ssd_scan(xbc, dt, a128, s0, *, bsz, lseq, tb, reverse, d):
    nblk = lseq // tb

    def blk(b, j):
        return b * nblk + ((nblk - 1 - j) if reverse else j)

    n_xs = D_INNER // (N_GROUPS * D_STATE)
    st_spec = pl.BlockSpec((1, N_GROUPS, D_STATE, GROUP_W), lambda b, j: (b, 0, 0, 0))
    return pl.pallas_call(
        functools.partial(_scan_kernel, reverse=reverse, d=d),
        grid=(bsz, nblk),
        in_specs=[pl.BlockSpec((tb, D_INNER), lambda b, j: (blk(b, j), 0)),
                  pl.BlockSpec((tb, N_GROUPS * D_STATE), lambda b, j: (blk(b, j), n_xs)),
                  pl.BlockSpec((tb, N_GROUPS * D_STATE), lambda b, j: (blk(b, j), n_xs + 1)),
                  pl.BlockSpec((tb, LANES), lambda b, j: (blk(b, j), 0)),
                  pl.BlockSpec((tb, LANES), lambda b, j: (blk(b, jnp.minimum(j + 1, nblk - 1)), 0)),
                  pl.BlockSpec((1, LANES), lambda b, j: (0, 0)),
                  st_spec],
        out_specs=[pl.BlockSpec((tb, D_INNER), lambda b, j: (blk(b, j), 0)), st_spec],
        out_shape=[jax.ShapeDtypeStruct((bsz * lseq, D_INNER), BF16),
                   jax.ShapeDtypeStruct((bsz, N_GROUPS, D_STATE, GROUP_W), F32)],
        scratch_shapes=[pltpu.VMEM((N_GROUPS, D_STATE, GROUP_W), F32),
                        pltpu.VMEM((2, LANES, CHUNK), F32), pltpu.VMEM((2, LANES, CHUNK), BF16)],
        compiler_params=_params(2), name="ssd_scan_bwd" if reverse else "ssd_scan_fwd",
    )(xbc, xbc, xbc, dt, dt, a128, s0)


MERGE_SUB = 256


def _merge_kernel(x_ref, h_ref, yf_ref, yb_ref, xs_ref, yc_ref, mod_ref, wz_ref, bz_ref, wg_ref, bg_ref,
                  dsk_ref, nrm_ref, wso_ref, wo_ref, o_ref, *, k_gate):
    tm = x_ref.shape[0]
    subs = [slice(r, r + MERGE_SUB) for r in range(0, tm, MERGE_SUB)]
    proj = []
    for rs in subs:
        hb = h_ref[rs, :]
        proj.append((_dot(hb, wz_ref[...]) + bz_ref[...], _dot(hb, wg_ref[...]) + bg_ref[...]))
    y_ssd = []
    for rs, (z, _) in zip(subs, proj):
        y = yf_ref[rs, :].astype(F32) + yb_ref[rs, :].astype(F32) + dsk_ref[...] * xs_ref[rs, :].astype(F32)
        y = y * (z * _sigmoid(z))
        parts = []
        for g in range(N_GROUPS):
            yg = y[:, g * GROUP_W:(g + 1) * GROUP_W]
            ms = jnp.mean(yg * yg, axis=-1, keepdims=True)
            parts.append(yg * lax.rsqrt(ms + EPS))
        yn = jnp.concatenate(parts, axis=1) * nrm_ref[...]
        y_ssd.append(_dot(yn.astype(BF16), wso_ref[...]))
    for rs, (_, graw), y_s in zip(subs, proj, y_ssd):
        m = _sigmoid(graw[:, :D_MODEL]) * yc_ref[rs, :].astype(F32) + _sigmoid(graw[:, D_MODEL:]) * y_s
        mix = _dot(m.astype(BF16), wo_ref[...])
        o_ref[rs, :] = x_ref[rs, :] + mod_ref[0, k_gate:k_gate + 1, :] * mix


def _merge(x, h, yf, yb, xbc, yc, mods, mod_map, wz, bz, wg, bg, dsk, nrm, wso, wo, *, tm, k_gate):
    t = x.shape[0]
    row = pl.BlockSpec((tm, D_MODEL), lambda i: (i, 0))
    wide = pl.BlockSpec((tm, D_INNER), lambda i: (i, 0))
    return pl.pallas_call(
        functools.partial(_merge_kernel, k_gate=k_gate),
        grid=(t // tm,),
        in_specs=[row, row, wide, wide, wide, row,
                  pl.BlockSpec((1, N_MOD, D_MODEL), mod_map),
                  _const_spec((D_MODEL, D_INNER)), _const_spec((1, D_INNER)),
                  _const_spec((D_MODEL, 2 * D_MODEL)), _const_spec((1, 2 * D_MODEL)),
                  _const_spec((1, D_INNER)), _const_spec((1, D_INNER)),
                  _const_spec((D_INNER, D_MODEL)), _const_spec((D_MODEL, D_MODEL))],
        out_specs=row,
        out_shape=jax.ShapeDtypeStruct((t, D_MODEL), F32),
        compiler_params=_params(), name="mixer_merge",
    )(x, h, yf, yb, xbc, yc, mods, wz, bz.reshape(1, -1), wg, bg.reshape(1, -1), dsk.reshape(1, -1),
      nrm.reshape(1, -1), wso, wo)


def kernel(x, c, ctx, c_ctx, w_ada, b_ada, ffn1_norm, ffn1_w_gate, ffn1_w_up, ffn1_w_down, mix_norm, w_in, b_in, conv_dw_w, conv_dw_b, conv_ln_g, conv_ln_b, conv_w_out, ssd_conv_w, ssd_conv_b, ssd_dt_bias, ssd_a_log, ssd_d, ssd_norm, ssd_w_out, w_out, ffn2_norm, ffn2_w_gate, ffn2_w_up, ffn2_w_down, final_norm):
    bsz, seq, _ = x.shape
    ctx_len = ctx.shape[1]
    depth = w_ada.shape[0]
    assert seq % 1024 == 0 and ctx_len % 256 == 0 and bsz + 1 <= 8

    tm_x, tm_c = 512, 256
    xf = x.reshape(bsz * seq, D_MODEL)
    cf = ctx.reshape(bsz * ctx_len, D_MODEL)
    tm_ffn = 2 * tm_x
    tm_ca = bsz * ctx_len
    assert tm_ca % MERGE_SUB == 0 and tm_ca <= tm_ffn
    x_map = lambda i: ((i * tm_x) // seq, 0, 0)
    x_map_ffn = lambda i: ((i * tm_ffn) // seq, 0, 0)
    c_map = lambda i: (bsz, 0, 0)

    cvec = jnp.zeros((8, D_MODEL), F32).at[:bsz].set(c).at[bsz].set(c_ctx)
    mods_all = _mods(cvec, w_ada, b_ada)[:, :bsz + 1].reshape(depth, bsz + 1, N_MOD, D_MODEL)

    s1 = 2 * CONV_DIM
    s2 = s1 + D_INNER
    s3 = s2 + XBC_DIM
    s4 = s3 + N_DIR * N_HEADS
    zero_state = jnp.zeros((bsz, N_GROUPS, D_STATE, GROUP_W), F32)

    for i in range(depth):
        last = i == depth - 1
        mods = mods_all[i]
        bf = lambda w: w.astype(BF16)
        w1 = (bf(ffn1_w_gate[i]), bf(ffn1_w_up[i]), bf(ffn1_w_down[i]))
        w2 = (bf(ffn2_w_gate[i]), bf(ffn2_w_up[i]), bf(ffn2_w_down[i]))
        wi, bi = w_in[i], b_in[i]
        wa, wb = bf(wi[:, :CONV_DIM]), bf(wi[:, CONV_DIM:s1])
        ba, bb = bi[:CONV_DIM], bi[CONV_DIM:s1]
        wz, bz = bf(wi[:, s1:s2]), bi[s1:s2]
        wx, bx = bf(wi[:, s2:s3]), bi[s2:s3]
        n_dt = N_DIR * N_HEADS
        wdt = bf(jnp.zeros((D_MODEL, LANES), F32).at[:, :n_dt].set(wi[:, s3:s4]))
        pad_dt = lambda v: jnp.zeros((LANES,), F32).at[:n_dt].set(v.reshape(-1))
        bdt = pad_dt(bi[s3:s4] + ssd_dt_bias[i].reshape(-1))
        a128 = pad_dt(-jnp.exp(ssd_a_log[i].astype(F32))).reshape(1, LANES)
        wgt, bgt = bf(wi[:, s4:]), bi[s4:]
        dsk = jnp.repeat(ssd_d[i], HEAD_DIM)
        conv_w = (wa, wb, ba, bb, conv_dw_w[i], conv_dw_b[i], conv_ln_g[i], conv_ln_b[i], bf(conv_w_out[i]))
        merge_w = (wz, bz, wgt, bgt, dsk, ssd_norm[i], bf(ssd_w_out[i]), bf(w_out[i]))
        feat_w = (wx, bx, wdt, bdt, ssd_conv_w[i], ssd_conv_b[i])

        xf, hx = _ffn(xf, mods, x_map_ffn, ffn1_norm[i], *w1, tm=tm_ffn, k0=0, post="mod", g2=mix_norm[i], k1=3)
        cf, hc = _ffn(cf, mods, c_map, ffn1_norm[i], *w1, tm=tm_ca, k0=0, post="mod", g2=mix_norm[i], k1=3)

        xbc_c, dt_c = _ssd_features(hc, *feat_w, tm=tm_c, lseq=ctx_len)
        xbc_x, dt_x = _ssd_features(hx, *feat_w, tm=tm_x, lseq=seq)
        scan_c = functools.partial(_ssd_scan, xbc_c, dt_c, a128, bsz=bsz, lseq=ctx_len, tb=256)
        scan_x = functools.partial(_ssd_scan, xbc_x, dt_x, a128, bsz=bsz, lseq=seq, tb=1024)
        yf_c, s_f = scan_c(zero_state, reverse=False, d=0)
        yb_c, s_b = scan_c(zero_state, reverse=True, d=1)
        yf_x, _ = scan_x(s_f, reverse=False, d=0)
        yb_x, _ = scan_x(s_b, reverse=True, d=1)

        yc_x = _conv_branch(hx, *conv_w, tm=tm_x, seg=GRID_W)
        xf = _merge(xf, hx, yf_x, yb_x, xbc_x, yc_x, mods, x_map, *merge_w, tm=tm_x, k_gate=5)
        if not last:
            yc_c = _conv_branch(hc, *conv_w, tm=tm_ca, seg=ctx_len)
            cf = _merge(cf, hc, yf_c, yb_c, xbc_c, yc_c, mods, c_map, *merge_w, tm=tm_ca, k_gate=5)

        if last:
            xf = _ffn(xf, mods, x_map_ffn, ffn2_norm[i], *w2, tm=tm_ffn, k0=6, post="final", g2=final_norm)
        else:
            xf = _ffn(xf, mods, x_map_ffn, ffn2_norm[i], *w2, tm=tm_ffn, k0=6)
            cf = _ffn(cf, mods, c_map, ffn2_norm[i], *w2, tm=tm_ca, k0=6)
    return xf.reshape(bsz, seq, D_MODEL)
```

```python
import functools

import jax
import jax.numpy as jnp
import numpy as np
from jax import lax
from jax.experimental import pallas as pl
from jax.experimental.pallas import tpu as pltpu

F32 = jnp.float32
BF16 = jnp.bfloat16

D_MODEL = 1024
N_MOD = 9
D_FF = 2816
CONV_DIM = D_MODEL
CONV_K = 31
D_INNER = 2 * D_MODEL
HEAD_DIM = 64
N_HEADS = D_INNER // HEAD_DIM
N_GROUPS = 8
HEADS_PER_GROUP = N_HEADS // N_GROUPS
GROUP_W = HEADS_PER_GROUP * HEAD_DIM
D_STATE = 128
SSD_CONV_K = 5
CHUNK = 128
N_DIR = 2
XBC_DIM = D_INNER + 2 * N_GROUPS * D_STATE
GRID_W = 64
EPS = 1e-6

LANES = 128
BF16_ROWS = 16
FF_CHUNK = 256
VMEM_LIMIT = 58 * 1024 * 1024


def _dot(a, b):
    return jnp.dot(a, b, preferred_element_type=F32)


LOG2E = 1.4426950408889634


def _sigmoid(v):
    return 1.0 / (1.0 + jnp.exp2(v * (-LOG2E)))


def _const_spec(shape):
    nd = len(shape)
    return pl.BlockSpec(shape, lambda *_: (0,) * nd, pipeline_mode=pl.Buffered(1))


def _params(n_axes=1, flags=None):
    return pltpu.CompilerParams(dimension_semantics=("arbitrary",) * n_axes,
                                vmem_limit_bytes=VMEM_LIMIT, flags=flags)


def _mods_kernel(c_ref, w_ref, b_ref, o_ref):
    c = c_ref[...]
    act = c * _sigmoid(c)
    o_ref[0] = jnp.dot(act, w_ref[0], precision=lax.Precision.HIGHEST,
                       preferred_element_type=F32) + b_ref[0]


def _mods(cvec, w_ada, b_ada):
    depth, _, n_out = w_ada.shape
    tn = D_MODEL
    return pl.pallas_call(
        _mods_kernel,
        grid=(depth, n_out // tn),
        in_specs=[pl.BlockSpec((8, D_MODEL), lambda l, j: (0, 0)),
                  pl.BlockSpec((1, D_MODEL, tn), lambda l, j: (l, 0, j)),
                  pl.BlockSpec((1, 1, tn), lambda l, j: (l, 0, j))],
        out_specs=pl.BlockSpec((1, 8, tn), lambda l, j: (l, 0, j)),
        out_shape=jax.ShapeDtypeStruct((depth, 8, n_out), F32),
        compiler_params=_params(2),
        name="ada_mods",
    )(cvec, w_ada, b_ada.reshape(depth, 1, n_out))


def _rms(v, g):
    ms = jnp.mean(v * v, axis=-1, keepdims=True)
    return v * lax.rsqrt(ms + EPS) * g


FFN_SUB = 512
FFN_LAG = 2


def _ffn_kernel(*refs, k0, post, k1):
    if post == "mod":
        x_ref, mod_ref, g_ref, wg_ref, wu_ref, wd_ref, g2_ref, o_ref, h_ref = refs
    elif post == "final":
        x_ref, mod_ref, g_ref, wg_ref, wu_ref, wd_ref, g2_ref, o_ref = refs
    else:
        x_ref, mod_ref, g_ref, wg_ref, wu_ref, wd_ref, o_ref = refs
    tm = x_ref.shape[0]
    shift = mod_ref[0, k0:k0 + 1, :]
    scale = mod_ref[0, k0 + 1:k0 + 2, :]
    gate = mod_ref[0, k0 + 2:k0 + 3, :]
    n_chunks = D_FF // FF_CHUNK
    subs = [slice(r, r + min(tm, FFN_SUB)) for r in range(0, tm, FFN_SUB)]

    def gate_up(hb, c):
        sl = slice(c * FF_CHUNK, (c + 1) * FF_CHUNK)
        return _dot(hb, wg_ref[:, sl]), _dot(hb, wu_ref[:, sl])

    def epilogue(rs, acc):
        y = x_ref[rs, :] + 0.5 * gate * acc
        if post == "final":
            o_ref[rs, :] = _rms(y, g2_ref[...])
            return
        o_ref[rs, :] = y
        if post == "mod":
            shift2 = mod_ref[0, k1:k1 + 1, :]
            scale2 = mod_ref[0, k1 + 1:k1 + 2, :]
            h_ref[rs, :] = (_rms(y, g2_ref[...]) * (1.0 + scale2) + shift2).astype(BF16)

    hbs, accs, pending = {}, {}, {}
    for t in range(n_chunks + FFN_LAG * (len(subs) - 1) + 1):
        for s, rs in enumerate(subs):
            c = t - FFN_LAG * s
            if c == 0:
                hbs[s] = (_rms(x_ref[rs, :], g_ref[...]) * (1.0 + scale) + shift).astype(BF16)
                accs[s] = jnp.zeros(hbs[s].shape, F32)
                pending[s] = gate_up(hbs[s], 0)
            if 0 <= c < n_chunks:
                g, u = pending[s]
                if c + 1 < n_chunks:
                    pending[s] = gate_up(hbs[s], c + 1)
                a = (g * _sigmoid(g)) * u
                accs[s] = accs[s] + _dot(a.astype(BF16), wd_ref[c * FF_CHUNK:(c + 1) * FF_CHUNK, :])
            if c == n_chunks:
                epilogue(rs, accs[s])


def _ffn(x, mods, mod_map, norm_g, wg, wu, wd, *, tm, k0, post="none", g2=None, k1=0):
    t = x.shape[0]
    row = pl.BlockSpec((tm, D_MODEL), lambda i: (i, 0))
    in_specs = [row,
                pl.BlockSpec((1, N_MOD, D_MODEL), mod_map),
                _const_spec((1, D_MODEL)),
                _const_spec((D_MODEL, D_FF)), _const_spec((D_MODEL, D_FF)),
                _const_spec((D_FF, D_MODEL))]
    args = [x, mods, norm_g.reshape(1, D_MODEL), wg, wu, wd]
    if post != "none":
        in_specs.append(_const_spec((1, D_MODEL)))
        args.append(g2.reshape(1, D_MODEL))
    if post == "mod":
        out_specs = [row, row]
        out_shape = [jax.ShapeDtypeStruct((t, D_MODEL), F32), jax.ShapeDtypeStruct((t, D_MODEL), BF16)]
    else:
        out_specs = row
        out_shape = jax.ShapeDtypeStruct((t, D_MODEL), F32)
    return pl.pallas_call(
        functools.partial(_ffn_kernel, k0=k0, post=post, k1=k1),
        grid=(t // tm,), in_specs=in_specs, out_specs=out_specs, out_shape=out_shape,
        compiler_params=_params(), name="half_ffn_" + post,
    )(*args)


CONV_K_PAD = 32
CONV_SUB = 256


def _split_bf16(m):
    hi = jnp.asarray(m, F32).astype(BF16)
    lo = (jnp.asarray(m, F32) - hi.astype(F32)).astype(BF16)
    return hi, lo


def _dft_consts(seg):
    n = 2 * seg
    t = np.arange(seg)
    r = np.arange(seg)
    th = 2.0 * np.pi * np.outer(r, t) / n
    fwd = np.concatenate([np.cos(th), np.sin(th)], axis=0)
    fwd[seg] = np.cos(np.pi * t)
    inv = np.concatenate([2.0 * np.cos(th.T), 2.0 * np.sin(th.T)], axis=1) / n
    inv[:, 0] = 1.0 / n
    inv[:, seg] = np.cos(np.pi * t) / n
    shift = CONV_K // 2 - np.arange(CONV_K)
    ph = 2.0 * np.pi * np.outer(r, shift) / n
    re, im = np.cos(ph), -np.sin(ph)
    re_nyq = re.copy()
    re_nyq[0] = np.cos(np.pi * shift)
    taps = np.zeros((3 * seg, CONV_K_PAD))
    taps[:, :CONV_K] = np.concatenate([re, im, re_nyq], axis=0)
    f2 = jnp.concatenate(_split_bf16(fwd), axis=1)
    g2 = jnp.concatenate(_split_bf16(inv), axis=1)
    return f2, g2, jnp.asarray(taps, F32)


def _conv_kernel(h_ref, wa_ref, wb_ref, ba_ref, bb_ref, dww_ref, dwb_ref, lng_ref, lnb_ref, wo_ref,
                 f2_ref, g2_ref, taps_ref, o_ref, coef_ref, conv_ref, *, seg):
    tm = h_ref.shape[0]

    @pl.when(pl.program_id(0) == 0)
    def _():
        coef_ref[...] = jnp.dot(taps_ref[...], dww_ref[...], precision=lax.Precision.HIGHEST,
                                preferred_element_type=F32)

    sub = max(seg, min(tm, CONV_SUB))
    subs = [slice(r, r + sub) for r in range(0, tm, sub)]
    glu = []
    for rs in subs:
        hb = h_ref[rs, :]
        glu.append((_dot(hb, wa_ref[...]) + ba_ref[...], _dot(hb, wb_ref[...]) + bb_ref[...]))
    specs = []
    for a, b in glu:
        ub = (a * _sigmoid(b)).astype(BF16)
        for s in range(sub // seg):
            us = ub[s * seg:(s + 1) * seg, :]
            specs.append(_dot(f2_ref[...], jnp.concatenate([us, us], axis=0)))
    h_re = coef_ref[0:seg, :]
    h_im = coef_ref[seg:2 * seg, :]
    h_re_nyq = coef_ref[2 * seg:3 * seg, :]
    for s, spec in enumerate(specs):
        co, si = spec[:seg, :], spec[seg:, :]
        y = jnp.concatenate([co * h_re + si * h_im, si * h_re_nyq - co * h_im], axis=0).astype(BF16)
        conv_ref[s * seg:(s + 1) * seg, :] = _dot(g2_ref[...], jnp.concatenate([y, y], axis=0)) + dwb_ref[...]

    for rs in subs:
        v = conv_ref[rs, :]
        mu = jnp.mean(v, axis=-1, keepdims=True)
        vc = v - mu
        var = jnp.mean(vc * vc, axis=-1, keepdims=True)
        vn = vc * lax.rsqrt(var + EPS) * lng_ref[...] + lnb_ref[...]
        act = vn * _sigmoid(vn)
        o_ref[rs, :] = _dot(act.astype(BF16), wo_ref[...]).astype(BF16)


def _conv_branch(h, wa, wb, ba, bb, dww, dwb, lng, lnb, wo, *, tm, seg):
    t = h.shape[0]
    assert 2 * seg >= seg + CONV_K - 1
    f2, g2, taps = _dft_consts(seg)
    dww_pad = jnp.zeros((CONV_K_PAD, CONV_DIM), F32).at[:CONV_K].set(dww)
    row = pl.BlockSpec((tm, D_MODEL), lambda i: (i, 0))
    vec = _const_spec((1, CONV_DIM))
    sq = _const_spec((D_MODEL, CONV_DIM))
    return pl.pallas_call(
        functools.partial(_conv_kernel, seg=seg),
        grid=(t // tm,),
        in_specs=[row, sq, sq, vec, vec, _const_spec((CONV_K_PAD, CONV_DIM)), vec, vec, vec, sq,
                  _const_spec(f2.shape), _const_spec(g2.shape), _const_spec(taps.shape)],
        out_specs=row,
        out_shape=jax.ShapeDtypeStruct((t, D_MODEL), BF16),
        scratch_shapes=[pltpu.VMEM((3 * seg, CONV_DIM), F32),
                        pltpu.VMEM((tm, CONV_DIM), F32)],
        compiler_params=_params(), name="conv_branch",
    )(h, wa, wb, ba.reshape(1, -1), bb.reshape(1, -1), dww_pad, dwb.reshape(1, -1), lng.reshape(1, -1),
      lnb.reshape(1, -1), wo, f2, g2, taps)


FEAT_ROWS = 64
FEAT_COLS = 1024
FEAT_STRIDE = 9
SUBLANES = 8
FEAT_SPAN = SUBLANES * FEAT_STRIDE


def _aligned(v, m):
    return v if isinstance(v, int) else pl.multiple_of(v, m)


def _feat_kernel(hp_ref, h_ref, hn_ref, wx_ref, bx_ref, wdt_ref, bdt_ref, cw_ref, cb_ref,
                 xbc_ref, dt_ref, hfull_ref, raw0_ref, raw1_ref, act0_ref, act1_ref, *, lseq):
    tm = h_ref.shape[0]
    halo = BF16_ROWS
    i = pl.program_id(0)
    hfull_ref[0:halo, :] = hp_ref[...]
    hfull_ref[halo:halo + tm, :] = h_ref[...]
    hfull_ref[halo + tm:, :] = hn_ref[...]
    t0 = i * tm
    seq_lo = (t0 // lseq) * lseq
    halo_iota = lax.broadcasted_iota(jnp.int32, (halo, 1), 0)
    keep_lo = (t0 - halo + halo_iota) >= seq_lo
    keep_hi = (t0 + tm + halo_iota) < seq_lo + lseq
    first = halo - SSD_CONV_K // 2
    n_lane_blk = FEAT_COLS // LANES
    n_span = tm // FEAT_SPAN

    def project(j, raw_ref):
        cols = pl.ds(_aligned(j * FEAT_COLS, FEAT_COLS), FEAT_COLS)
        raw = _dot(hfull_ref[...], wx_ref[:, cols]) + bx_ref[:, cols]
        for lb in range(n_lane_blk):
            part = raw[:, lb * LANES:(lb + 1) * LANES]
            raw_ref[lb, 0:halo, :] = jnp.where(keep_lo, part[0:halo, :], 0.0)
            raw_ref[lb, halo:halo + tm, :] = part[halo:halo + tm, :]
            raw_ref[lb, halo + tm:, :] = jnp.where(keep_hi, part[halo + tm:, :], 0.0)

    def conv(j, raw_ref, act_ref):
        for lb in range(n_lane_blk):
            dst = pl.ds(_aligned(j * FEAT_COLS + lb * LANES, LANES), LANES)
            taps = [cw_ref[k:k + 1, dst] for k in range(SSD_CONV_K)]
            bias = cb_ref[:, dst]
            for sp in range(n_span):
                base = sp * FEAT_SPAN
                srcs = [raw_ref[lb, pl.ds(base + first + o, SUBLANES, stride=FEAT_STRIDE), :]
                        for o in range(FEAT_STRIDE + SSD_CONV_K - 1)]
                for r in range(FEAT_STRIDE):
                    acc = bias
                    for k in range(SSD_CONV_K):
                        acc = acc + srcs[r + k] * taps[k]
                    act_ref[lb, pl.ds(base + r, SUBLANES, stride=FEAT_STRIDE), :] = acc * _sigmoid(acc)
            for r0 in range(n_span * FEAT_SPAN, tm, FEAT_ROWS):
                nr = min(FEAT_ROWS, tm - r0)
                acc = bias
                for k in range(SSD_CONV_K):
                    acc = acc + raw_ref[lb, r0 + first + k:r0 + first + k + nr, :] * taps[k]
                act_ref[lb, r0:r0 + nr, :] = acc * _sigmoid(acc)
            xbc_ref[:, dst] = act_ref[lb].astype(BF16)

    n_blk = XBC_DIM // FEAT_COLS
    raws, acts = (raw0_ref, raw1_ref), (act0_ref, act1_ref)
    project(0, raws[0])
    for j in range(n_blk):
        if j + 1 < n_blk:
            project(j + 1, raws[(j + 1) % 2])
        conv(j, raws[j % 2], acts[j % 2])

    dtr = _dot(h_ref[...], wdt_ref[...]) + bdt_ref[...]
    sp = jnp.maximum(dtr, 0.0) + jnp.log1p(jnp.exp(-jnp.abs(dtr)))
    lane = lax.broadcasted_iota(jnp.int32, (1, LANES), 1)
    dt_ref[...] = jnp.where(lane < N_DIR * N_HEADS, sp, 0.0)


def _ssd_features(h, wx, bx, wdt, bdt, cw, cb, *, tm, lseq):
    t = h.shape[0]
    nb = tm // BF16_ROWS
    last = t // BF16_ROWS - 1
    return pl.pallas_call(
        functools.partial(_feat_kernel, lseq=lseq),
        grid=(t // tm,),
        in_specs=[pl.BlockSpec((BF16_ROWS, D_MODEL), lambda i: (jnp.maximum(i * nb - 1, 0), 0)),
                  pl.BlockSpec((tm, D_MODEL), lambda i: (i, 0)),
                  pl.BlockSpec((BF16_ROWS, D_MODEL), lambda i: (jnp.minimum((i + 1) * nb, last), 0)),
                  _const_spec((D_MODEL, XBC_DIM)), _const_spec((1, XBC_DIM)),
                  _const_spec((D_MODEL, LANES)), _const_spec((1, LANES)),
                  _const_spec((SSD_CONV_K, XBC_DIM)), _const_spec((1, XBC_DIM))],
        out_specs=[pl.BlockSpec((tm, XBC_DIM), lambda i: (i, 0)),
                   pl.BlockSpec((tm, LANES), lambda i: (i, 0))],
        out_shape=[jax.ShapeDtypeStruct((t, XBC_DIM), BF16), jax.ShapeDtypeStruct((t, LANES), F32)],
        scratch_shapes=[pltpu.VMEM((tm + 2 * BF16_ROWS, D_MODEL), BF16),
                        pltpu.VMEM((FEAT_COLS // LANES, tm + 2 * BF16_ROWS, LANES), F32),
                        pltpu.VMEM((FEAT_COLS // LANES, tm + 2 * BF16_ROWS, LANES), F32),
                        pltpu.VMEM((FEAT_COLS // LANES, tm, LANES), F32),
                        pltpu.VMEM((FEAT_COLS // LANES, tm, LANES), F32)],
        compiler_params=_params(), name="ssd_features",
    )(h, h, h, wx, bx.reshape(1, -1), wdt, bdt.reshape(1, -1), cw, cb.reshape(1, -1))


def _scan_kernel(xs_ref, bm_ref, cm_ref, dt_ref, dtn_ref, a_ref, s0_ref, y_ref, sf_ref, state_ref, tf_ref, tb_ref,
                 *, reverse, d):
    tb = xs_ref.shape[0]
    nch = tb // CHUNK
    j = pl.program_id(1)

    @pl.when(j == 0)
    def _():
        state_ref[...] = s0_ref[0]

    row_i = lax.broadcasted_iota(jnp.int32, (CHUNK, CHUNK), 0)
    col_i = lax.broadcasted_iota(jnp.int32, (CHUNK, CHUNK), 1)
    causal = (col_i >= row_i) if reverse else (row_i >= col_i)
    tri = causal.astype(F32)
    edge = 0 if reverse else CHUNK - 1
    lo64 = lax.broadcasted_iota(jnp.int32, (1, LANES), 1) < HEAD_DIM

    def pair(v0, v1):
        return jnp.where(lo64, v0, v1)

    def chunk_rows(c):
        ci = (nch - 1 - c) if reverse else c
        return pl.ds(_aligned(ci * CHUNK, CHUNK), CHUNK)

    def decay_terms(dt):
        acum = jnp.dot(tri, dt * (a_ref[...] * LOG2E), precision=lax.Precision.HIGHEST,
                       preferred_element_type=F32)
        acum_t = acum.T
        dt_t = dt.T
        a_edge_t = jnp.broadcast_to(acum_t[:, edge:edge + 1], (LANES, CHUNK))
        w_t = (jnp.exp2(a_edge_t - acum_t) * dt_t).astype(BF16)
        return acum, acum_t, w_t, dt_t.astype(BF16)

    def chunk_work(c, terms):
        rows = chunk_rows(c)
        acum, acum_t, w_t, dt_tb = terms
        zeros_half = jnp.zeros((CHUNK, LANES), BF16)
        staged = []
        for g in range(N_GROUPS):
            bg = bm_ref[rows, g * D_STATE:(g + 1) * D_STATE]
            cg = cm_ref[rows, g * D_STATE:(g + 1) * D_STATE]
            bg_t = bg.astype(F32).T.astype(BF16)
            cb = jnp.where(causal, _dot(cg, bg_t), 0.0).astype(BF16)
            s_in = state_ref[g]
            y_off = _dot(cg, s_in.astype(BF16))
            m_parts, bw_parts, a_cols, rhs_parts = [], [], [], []
            for hh in range(HEADS_PER_GROUP):
                q = d * N_HEADS + g * HEADS_PER_GROUP + hh
                a_col = jnp.broadcast_to(acum[:, q:q + 1], (CHUNK, CHUNK))
                decay = jnp.exp2(jnp.minimum(a_col - acum_t[q:q + 1, :], 0.0))
                m_parts.append(cb * decay.astype(BF16) * dt_tb[q:q + 1, :])
                bw_parts.append(bg_t * w_t[q:q + 1, :])
                a_cols.append(a_col)
                half = xs_ref[rows, g * GROUP_W + (hh // 2) * LANES:g * GROUP_W + (hh // 2 + 1) * LANES]
                own = jnp.where(lo64 if hh % 2 == 0 else jnp.logical_not(lo64), half, zeros_half)
                rhs_parts.append(jnp.concatenate([own, zeros_half] if hh < 2 else [zeros_half, own], axis=1))
            lhs = jnp.concatenate([jnp.concatenate(m_parts, axis=1), jnp.concatenate(bw_parts, axis=1)], axis=0)
            e_grp = jnp.exp2(jnp.concatenate([pair(a_cols[0], a_cols[1]), pair(a_cols[2], a_cols[3])], axis=1))
            staged.append((lhs, jnp.concatenate(rhs_parts, axis=0), s_in, y_off, e_grp))
        for g, (lhs, rhs, s_in, y_off, e_grp) in enumerate(staged):
            res = _dot(lhs, rhs)
            y = res[:CHUNK, :] + y_off * e_grp
            y_ref[rows, g * GROUP_W:(g + 1) * GROUP_W] = y.astype(BF16)
            state_ref[g] = s_in * e_grp[edge:edge + 1, :] + res[CHUNK:, :]

    def save_terms(terms):
        acum, acum_t, w_t, dt_tb = terms
        tf_ref[0], tf_ref[1], tb_ref[0], tb_ref[1] = acum, acum_t, w_t, dt_tb

    @pl.when(j == 0)
    def _():
        save_terms(decay_terms(dt_ref[chunk_rows(0), :]))

    def loop_body(c, terms):
        next_terms = decay_terms(dt_ref[chunk_rows(c + 1), :])
        chunk_work(c, terms)
        return next_terms

    terms = lax.fori_loop(0, nch - 1, loop_body, (tf_ref[0], tf_ref[1], tb_ref[0], tb_ref[1]))
    next_terms = decay_terms(dtn_ref[chunk_rows(0), :])
    chunk_work(nch - 1, terms)
    save_terms(next_terms)

    @pl.when(j == pl.num_programs(1) - 1)
    def _():
        sf_ref[0] = state_ref[...]


def _ssd_scan(xbc, dt, a128, s0, *, bsz, lseq, tb, reverse, d):
    nblk = lseq // tb

    def blk(b, j):
        return b * nblk + ((nblk - 1 - j) if reverse else j)

    n_xs = D_INNER // (N_GROUPS * D_STATE)
    st_spec = pl.BlockSpec((1, N_GROUPS, D_STATE, GROUP_W), lambda b, j: (b, 0, 0, 0))
    return pl.pallas_call(
        functools.partial(_scan_kernel, reverse=reverse, d=d),
        grid=(bsz, nblk),
        in_specs=[pl.BlockSpec((tb, D_INNER), lambda b, j: (blk(b, j), 0)),
                  pl.BlockSpec((tb, N_GROUPS * D_STATE), lambda b, j: (blk(b, j), n_xs)),
                  pl.BlockSpec((tb, N_GROUPS * D_STATE), lambda b, j: (blk(b, j), n_xs + 1)),
                  pl.BlockSpec((tb, LANES), lambda b, j: (blk(b, j), 0)),
                  pl.BlockSpec((tb, LANES), lambda b, j: (blk(b, jnp.minimum(j + 1, nblk - 1)), 0)),
                  pl.BlockSpec((1, LANES), lambda b, j: (0, 0)),
                  st_spec],
        out_specs=[pl.BlockSpec((tb, D_INNER), lambda b, j: (blk(b, j), 0)), st_spec],
        out_shape=[jax.ShapeDtypeStruct((bsz * lseq, D_INNER), BF16),
                   jax.ShapeDtypeStruct((bsz, N_GROUPS, D_STATE, GROUP_W), F32)],
        scratch_shapes=[pltpu.VMEM((N_GROUPS, D_STATE, GROUP_W), F32),
                        pltpu.VMEM((2, LANES, CHUNK), F32), pltpu.VMEM((2, LANES, CHUNK), BF16)],
        compiler_params=_params(2), name="ssd_scan_bwd" if reverse else "ssd_scan_fwd",
    )(xbc, xbc, xbc, dt, dt, a128, s0)


MERGE_SUB = 256


def _merge_kernel(x_ref, h_ref, yf_ref, yb_ref, xs_ref, yc_ref, mod_ref, wz_ref, bz_ref, wg_ref, bg_ref,
                  dsk_ref, nrm_ref, wso_ref, wo_ref, o_ref, *, k_gate):
    tm = x_ref.shape[0]
    subs = [slice(r, r + MERGE_SUB) for r in range(0, tm, MERGE_SUB)]
    proj = []
    for rs in subs:
        hb = h_ref[rs, :]
        proj.append((_dot(hb, wz_ref[...]) + bz_ref[...], _dot(hb, wg_ref[...]) + bg_ref[...]))
    y_ssd = []
    for rs, (z, _) in zip(subs, proj):
        y = yf_ref[rs, :].astype(F32) + yb_ref[rs, :].astype(F32) + dsk_ref[...] * xs_ref[rs, :].astype(F32)
        y = y * (z * _sigmoid(z))
        parts = []
        for g in range(N_GROUPS):
            yg = y[:, g * GROUP_W:(g + 1) * GROUP_W]
            ms = jnp.mean(yg * yg, axis=-1, keepdims=True)
            parts.append(yg * lax.rsqrt(ms + EPS))
        yn = jnp.concatenate(parts, axis=1) * nrm_ref[...]
        y_ssd.append(_dot(yn.astype(BF16), wso_ref[...]))
    for rs, (_, graw), y_s in zip(subs, proj, y_ssd):
        m = _sigmoid(graw[:, :D_MODEL]) * yc_ref[rs, :].astype(F32) + _sigmoid(graw[:, D_MODEL:]) * y_s
        mix = _dot(m.astype(BF16), wo_ref[...])
        o_ref[rs, :] = x_ref[rs, :] + mod_ref[0, k_gate:k_gate + 1, :] * mix


def _merge(x, h, yf, yb, xbc, yc, mods, mod_map, wz, bz, wg, bg, dsk, nrm, wso, wo, *, tm, k_gate):
    t = x.shape[0]
    row = pl.BlockSpec((tm, D_MODEL), lambda i: (i, 0))
    wide = pl.BlockSpec((tm, D_INNER), lambda i: (i, 0))
    return pl.pallas_call(
        functools.partial(_merge_kernel, k_gate=k_gate),
        grid=(t // tm,),
        in_specs=[row, row, wide, wide, wide, row,
                  pl.BlockSpec((1, N_MOD, D_MODEL), mod_map),
                  _const_spec((D_MODEL, D_INNER)), _const_spec((1, D_INNER)),
                  _const_spec((D_MODEL, 2 * D_MODEL)), _const_spec((1, 2 * D_MODEL)),
                  _const_spec((1, D_INNER)), _const_spec((1, D_INNER)),
                  _const_spec((D_INNER, D_MODEL)), _const_spec((D_MODEL, D_MODEL))],
        out_specs=row,
        out_shape=jax.ShapeDtypeStruct((t, D_MODEL), F32),
        compiler_params=_params(), name="mixer_merge",
    )(x, h, yf, yb, xbc, yc, mods, wz, bz.reshape(1, -1), wg, bg.reshape(1, -1), dsk.reshape(1, -1),
      nrm.reshape(1, -1), wso, wo)


def kernel(x, c, ctx, c_ctx, w_ada, b_ada, ffn1_norm, ffn1_w_gate, ffn1_w_up, ffn1_w_down, mix_norm, w_in, b_in, conv_dw_w, conv_dw_b, conv_ln_g, conv_ln_b, conv_w_out, ssd_conv_w, ssd_conv_b, ssd_dt_bias, ssd_a_log, ssd_d, ssd_norm, ssd_w_out, w_out, ffn2_norm, ffn2_w_gate, ffn2_w_up, ffn2_w_down, final_norm):
    bsz, seq, _ = x.shape
    ctx_len = ctx.shape[1]
    depth = w_ada.shape[0]
    assert seq % 1024 == 0 and ctx_len % 256 == 0 and bsz + 1 <= 8

    tm_x, tm_c = 512, 256
    xf = x.reshape(bsz * seq, D_MODEL)
    cf = ctx.reshape(bsz * ctx_len, D_MODEL)
    tm_ffn = 2 * tm_x
    tm_ca = bsz * ctx_len
    assert tm_ca % MERGE_SUB == 0 and tm_ca <= tm_ffn
    x_map = lambda i: ((i * tm_x) // seq, 0, 0)
    x_map_ffn = lambda i: ((i * tm_ffn) // seq, 0, 0)
    c_map = lambda i: (bsz, 0, 0)

    cvec = jnp.zeros((8, D_MODEL), F32).at[:bsz].set(c).at[bsz].set(c_ctx)
    mods_all = _mods(cvec, w_ada, b_ada)[:, :bsz + 1].reshape(depth, bsz + 1, N_MOD, D_MODEL)

    s1 = 2 * CONV_DIM
    s2 = s1 + D_INNER
    s3 = s2 + XBC_DIM
    s4 = s3 + N_DIR * N_HEADS
    zero_state = jnp.zeros((bsz, N_GROUPS, D_STATE, GROUP_W), F32)

    for i in range(depth):
        last = i == depth - 1
        mods = mods_all[i]
        bf = lambda w: w.astype(BF16)
        w1 = (bf(ffn1_w_gate[i]), bf(ffn1_w_up[i]), bf(ffn1_w_down[i]))
        w2 = (bf(ffn2_w_gate[i]), bf(ffn2_w_up[i]), bf(ffn2_w_down[i]))
        wi, bi = w_in[i], b_in[i]
        wa, wb = bf(wi[:, :CONV_DIM]), bf(wi[:, CONV_DIM:s1])
        ba, bb = bi[:CONV_DIM], bi[CONV_DIM:s1]
        wz, bz = bf(wi[:, s1:s2]), bi[s1:s2]
        wx, bx = bf(wi[:, s2:s3]), bi[s2:s3]
        n_dt = N_DIR * N_HEADS
        wdt = bf(jnp.zeros((D_MODEL, LANES), F32).at[:, :n_dt].set(wi[:, s3:s4]))
        pad_dt = lambda v: jnp.zeros((LANES,), F32).at[:n_dt].set(v.reshape(-1))
        bdt = pad_dt(bi[s3:s4] + ssd_dt_bias[i].reshape(-1))
        a128 = pad_dt(-jnp.exp(ssd_a_log[i].astype(F32))).reshape(1, LANES)
        wgt, bgt = bf(wi[:, s4:]), bi[s4:]
        dsk = jnp.repeat(ssd_d[i], HEAD_DIM)
        conv_w = (wa, wb, ba, bb, conv_dw_w[i], conv_dw_b[i], conv_ln_g[i], conv_ln_b[i], bf(conv_w_out[i]))
        merge_w = (wz, bz, wgt, bgt, dsk, ssd_norm[i], bf(ssd_w_out[i]), bf(w_out[i]))
        feat_w = (wx, bx, wdt, bdt, ssd_conv_w[i], ssd_conv_b[i])

        xf, hx = _ffn(xf, mods, x_map_ffn, ffn1_norm[i], *w1, tm=tm_ffn, k0=0, post="mod", g2=mix_norm[i], k1=3)
        cf, hc = _ffn(cf, mods, c_map, ffn1_norm[i], *w1, tm=tm_ca, k0=0, post="mod", g2=mix_norm[i], k1=3)

        xbc_c, dt_c = _ssd_features(hc, *feat_w, tm=tm_c, lseq=ctx_len)
        xbc_x, dt_x = _ssd_features(hx, *feat_w, tm=tm_x, lseq=seq)
        scan_c = functools.partial(_ssd_scan, xbc_c, dt_c, a128, bsz=bsz, lseq=ctx_len, tb=256)
        scan_x = functools.partial(_ssd_scan, xbc_x, dt_x, a128, bsz=bsz, lseq=seq, tb=1024)
        yf_c, s_f = scan_c(zero_state, reverse=False, d=0)
        yb_c, s_b = scan_c(zero_state, reverse=True, d=1)
        yf_x, _ = scan_x(s_f, reverse=False, d=0)
        yb_x, _ = scan_x(s_b, reverse=True, d=1)

        yc_x = _conv_branch(hx, *conv_w, tm=tm_x, seg=GRID_W)
        xf = _merge(xf, hx, yf_x, yb_x, xbc_x, yc_x, mods, x_map, *merge_w, tm=tm_x, k_gate=5)
        if not last:
            yc_c = _conv_branch(hc, *conv_w, tm=tm_ca, seg=ctx_len)
            cf = _merge(cf, hc, yf_c, yb_c, xbc_c, yc_c, mods, c_map, *merge_w, tm=tm_ca, k_gate=5)

        if last:
            xf = _ffn(xf, mods, x_map_ffn, ffn2_norm[i], *w2, tm=tm_ffn, k0=6, post="final", g2=final_norm)
        else:
            xf = _ffn(xf, mods, x_map_ffn, ffn2_norm[i], *w2, tm=tm_ffn, k0=6)
            cf = _ffn(cf, mods, c_map, ffn2_norm[i], *w2, tm=tm_ca, k0=6)
    return xf.reshape(bsz, seq, D_MODEL)
```
